```python
import math
import jax, jax.numpy as jnp
from jax import lax
import numpy as np

D_MODEL = 1024
BATCH = 16
SEQ = 256
DEPTH = 4
DEC_BATCH = 4
DEC_SEQ = 1024
PAST_LEN = 256

GRID_W = 64
HEAD_DIM = 64
BRANCH_W = 512
N_BRANCH = 3
SSD_HEADS = 8
SSD_HEAD_DIM = 64
SSD_INNER = SSD_HEADS * SSD_HEAD_DIM
SSD_GROUPS = 2
SSD_STATE = 64
SSD_BC = SSD_GROUPS * SSD_STATE
SSD_CHUNK = 128
CONV_W = 5
CONV_CH = SSD_INNER + 2 * SSD_BC
NA_HEADS = 8
NA_ROWS = 8
NA_COLS = 16
NA_SLAB = 2 * NA_COLS
GQA_HEADS = 8
GQA_KV_HEADS = 2
ROPE_THETA = 10000.0
Q_BLOCK = 128
PEER_HEADS = 8
PEER_KEYS = 128
PEER_EXPERTS = PEER_KEYS * PEER_KEYS
PEER_QDIM = 256
PEER_TOPK = 16
PEER_TOK_BLOCK = 128
DN_ALPHA = (2 * DEPTH) ** 0.25
DN_BETA = (8 * DEPTH) ** -0.25
EPS = 1e-6
IN_SIZES = (SSD_INNER, SSD_INNER, SSD_BC, SSD_BC, 2 * SSD_HEADS,
            NA_HEADS * HEAD_DIM, NA_HEADS * HEAD_DIM, NA_HEADS * HEAD_DIM,
            GQA_HEADS * HEAD_DIM, GQA_KV_HEADS * HEAD_DIM, GQA_KV_HEADS * HEAD_DIM,
            N_BRANCH * D_MODEL)
IN_COLS = sum(IN_SIZES)

kernel_name = 'hybrid_dit_ssd_na_gqa_peer_step'


def layer_norm(x, g=None, b=None):
    xf = x.astype(jnp.float32)
    mu = jnp.mean(xf, -1, keepdims=True)
    var = jnp.mean(jnp.square(xf - mu), -1, keepdims=True)
    y = ((xf - mu) * lax.rsqrt(var + EPS)).astype(x.dtype)
    if g is not None:
        y = y * g + b
    return y


def rms_norm(x, g):
    xf = x.astype(jnp.float32)
    y = xf * lax.rsqrt(jnp.mean(xf * xf, -1, keepdims=True) + EPS)
    return y.astype(x.dtype) * g


def adaln(cond, w, b):
    m = jax.nn.silu(cond) @ w + b
    return jnp.split(m[..., None, :], 6, axis=-1)


def modulate(x, shift, scale):
    return layer_norm(x) * (1 + scale) + shift


def dwconv(x, w, b):
    y = lax.conv_general_dilated(x, w[:, None, :].astype(x.dtype), window_strides=(1,),
                                 padding=[(CONV_W // 2, CONV_W // 2)],
                                 dimension_numbers=('NWC', 'WIO', 'NWC'),
                                 feature_group_count=x.shape[-1])
    return jax.nn.silu(y + b)


def segsum(a):
    cs = jnp.cumsum(a, axis=-1)
    diff = cs[..., :, None] - cs[..., None, :]
    n = a.shape[-1]
    return jnp.where(jnp.tril(jnp.ones((n, n), bool)), diff, -jnp.inf)


def ssd_scan(x, dt, A, B, C, h0):
    b, L, H, P = x.shape
    N = B.shape[-1]
    nc = L // SSD_CHUNK
    Q = SSD_CHUNK
    xd = (x * dt[..., None]).reshape(b, nc, Q, H, P)
    a = (dt * A).reshape(b, nc, Q, H).transpose(0, 3, 1, 2)
    Bc = B.reshape(b, nc, Q, H, N)
    Cc = C.reshape(b, nc, Q, H, N)
    a_cs = jnp.cumsum(a, axis=-1)
    Lmat = jnp.exp(segsum(a))
    cb = jnp.einsum('bclhn,bcshn->bhcls', Cc, Bc)
    y_diag = jnp.einsum('bhcls,bcshp->bclhp', cb * Lmat, xd)
    decay_states = jnp.exp(a_cs[..., -1:] - a_cs)
    states = jnp.einsum('bclhn,bhcl,bclhp->bchpn', Bc, decay_states, xd)
    states = jnp.concatenate([h0[:, None], states], axis=1)
    chunk_a = jnp.pad(a_cs[..., -1], ((0, 0), (0, 0), (1, 0)))
    decay_chunk = jnp.exp(segsum(chunk_a))
    new_states = jnp.einsum('bhzc,bchpn->bzhpn', decay_chunk, states)
    states, final = new_states[:, :-1], new_states[:, -1]
    y_off = jnp.einsum('bclhn,bchpn,bhcl->bclhp', Cc, states, jnp.exp(a_cs))
    return (y_diag + y_off).reshape(b, L, H, P), final


def ssd_mixer(z, xbc, dt_raw, a_log, dt_bias, d_skip, norm_g, h0):
    b, L, _ = z.shape
    xs, Bm, Cm = jnp.split(xbc.astype(jnp.float32), [SSD_INNER, SSD_INNER + SSD_BC], axis=-1)
    xh = xs.reshape(b, L, SSD_HEADS, SSD_HEAD_DIM)
    rep = SSD_HEADS // SSD_GROUPS
    Bh = jnp.repeat(Bm.reshape(b, L, SSD_GROUPS, SSD_STATE), rep, axis=2)
    Ch = jnp.repeat(Cm.reshape(b, L, SSD_GROUPS, SSD_STATE), rep, axis=2)
    dt = jax.nn.softplus(dt_raw.astype(jnp.float32).reshape(b, L, 2, SSD_HEADS) + dt_bias.astype(jnp.float32))
    A = -jnp.exp(a_log.astype(jnp.float32))
    h0 = h0.astype(jnp.float32)
    y_f, s_f = ssd_scan(xh, dt[:, :, 0], A[0], Bh, Ch, h0[:, 0])
    fl = lambda t: jnp.flip(t, axis=1)
    y_b, s_b = ssd_scan(fl(xh), fl(dt[:, :, 1]), A[1], fl(Bh), fl(Ch), h0[:, 1])
    y = y_f + fl(y_b) + d_skip.astype(jnp.float32)[:, None] * xh
    y = y.reshape(b, L, SSD_INNER) * jax.nn.silu(z.astype(jnp.float32))
    y = rms_norm(y, norm_g)
    return y.astype(z.dtype), jnp.stack([s_f, s_b], axis=1).astype(z.dtype)


def blocked_attention(q, k, v):
    b, Lq, H, d = q.shape
    kvh = k.shape[2]
    g = H // kvh
    nb = Lq // Q_BLOCK
    qb = q.reshape(b, nb, Q_BLOCK, kvh, g, d).transpose(1, 0, 2, 3, 4, 5)
    scale = d ** -0.5

    def one_block(qi):
        s = jnp.einsum('bqkgd,bskd->bkgqs', qi, k).astype(jnp.float32) * scale
        p = jax.nn.softmax(s, axis=-1).astype(v.dtype)
        return jnp.einsum('bkgqs,bskd->bqkgd', p, v)

    o = lax.map(one_block, qb)
    return o.transpose(1, 0, 2, 3, 4, 5).reshape(b, Lq, H * d)


def axial_rope(x):
    L = x.shape[1]
    t = jnp.arange(L)
    row = (t // GRID_W).astype(jnp.float32)
    col = (t % GRID_W).astype(jnp.float32)
    half = x.shape[-1] // 2
    inv = 1.0 / (ROPE_THETA ** (jnp.arange(0, half, 2, dtype=jnp.float32) / half))

    def rot(xh, pos):
        ang = pos[:, None] * inv
        cos = jnp.cos(ang)[None, :, None, :]
        sin = jnp.sin(ang)[None, :, None, :]
        x1, x2 = jnp.split(xh, 2, axis=-1)
        return jnp.concatenate([x1 * cos - x2 * sin, x2 * cos + x1 * sin], axis=-1)

    xr, xc = jnp.split(x.astype(jnp.float32), 2, axis=-1)
    return jnp.concatenate([rot(xr, row), rot(xc, col)], axis=-1).astype(x.dtype)


def neighbourhood_attention(q, k, v, ctx_k, ctx_v, rpb):
    b, L, H, d = q.shape
    rows = L // GRID_W
    kr = min(NA_ROWS, rows)
    ncb = GRID_W // NA_COLS
    r = jnp.arange(rows)
    key_rows = jnp.clip(r - kr // 2, 0, rows - kr)[:, None] + jnp.arange(kr)
    slab0 = jnp.clip(jnp.arange(ncb) * NA_COLS - NA_COLS // 2, 0, GRID_W - NA_SLAB)
    key_cols = slab0[:, None] + jnp.arange(NA_SLAB)
    qcol = jnp.arange(ncb)[:, None] * NA_COLS + jnp.arange(NA_COLS)
    qstart = jnp.clip(qcol - NA_COLS // 2, 0, GRID_W - NA_COLS)
    kc = key_cols[:, None, :]
    valid = (kc >= qstart[..., None]) & (kc < qstart[..., None] + NA_COLS)
    dr_idx = key_rows - r[:, None] + NA_ROWS - 1
    dc_idx = jnp.clip(kc - qcol[..., None] + NA_COLS - 1, 0, 2 * NA_COLS - 2)
    bias = rpb[:, dr_idx[:, None, None, :, None], dc_idx[None, :, :, None, :]].astype(jnp.float32)
    kg = k.reshape(b, rows, GRID_W, H, d)
    vg = v.reshape(b, rows, GRID_W, H, d)
    ri = key_rows[:, None, :, None]
    ci = key_cols[None, :, None, :]
    kb = kg[:, ri, ci]
    vb = vg[:, ri, ci].reshape(b, rows, ncb, kr * NA_SLAB, H, d)
    qb = q.reshape(b, rows, ncb, NA_COLS, H, d)
    scale = d ** -0.5
    s_loc = jnp.einsum('brcqhd,brcijhd->bhrcqij', qb, kb).astype(jnp.float32) * scale + bias[None]
    s_loc = jnp.where(valid[:, :, None, :], s_loc, -jnp.inf)
    n_loc = kr * NA_SLAB
    s_loc = s_loc.reshape(b, H, rows, ncb, NA_COLS, n_loc)
    s_ctx = jnp.einsum('brcqhd,bshd->bhrcqs', qb, ctx_k).astype(jnp.float32) * scale
    p = jax.nn.softmax(jnp.concatenate([s_loc, s_ctx], axis=-1), axis=-1).astype(v.dtype)
    o = (jnp.einsum('bhrcqn,brcnhd->brcqhd', p[..., :n_loc], vb)
         + jnp.einsum('bhrcqs,bshd->brcqhd', p[..., n_loc:], ctx_v))
    return o.reshape(b, L, H * d)


def token_mixer(h, w_in, conv_w, conv_b, a_log, dt_bias, d_skip, ssd_g, rpb, qn_g, kn_g, w_branch, w_out, ctx):
    b, L, _ = h.shape
    p = h @ w_in
    offs = np.cumsum(IN_SIZES)[:-1].tolist()
    z, xs, Bm, Cm, dt_raw, na_q, na_k, na_v, g_q, g_k, g_v, gate_raw = jnp.split(p, offs, axis=-1)
    xbc = dwconv(jnp.concatenate([xs, Bm, Cm], axis=-1), conv_w, conv_b)
    na_q = na_q.reshape(b, L, NA_HEADS, HEAD_DIM)
    na_k = na_k.reshape(b, L, NA_HEADS, HEAD_DIM)
    na_v = na_v.reshape(b, L, NA_HEADS, HEAD_DIM)
    g_q = rms_norm(g_q.reshape(b, L, GQA_HEADS, HEAD_DIM), qn_g)
    g_k = rms_norm(g_k.reshape(b, L, GQA_KV_HEADS, HEAD_DIM), kn_g)
    g_v = g_v.reshape(b, L, GQA_KV_HEADS, HEAD_DIM)
    if ctx is None:
        h0 = jnp.zeros((b, 2, SSD_HEADS, SSD_HEAD_DIM, SSD_STATE), h.dtype)
        y_a, ssd_state = ssd_mixer(z, xbc, dt_raw, a_log, dt_bias, d_skip, ssd_g, h0)
        y_b = blocked_attention(na_q, na_k, na_v)
        y_c = blocked_attention(g_q, g_k, g_v)
        new_ctx = (na_k, na_v, g_k, g_v, ssd_state)
    else:
        c_na_k, c_na_v, c_g_k, c_g_v, c_ssd = ctx
        y_a, _ = ssd_mixer(z, xbc, dt_raw, a_log, dt_bias, d_skip, ssd_g, c_ssd)
        y_b = neighbourhood_attention(na_q, na_k, na_v, c_na_k, c_na_v, rpb)
        k_all = jnp.concatenate([c_g_k, axial_rope(g_k)], axis=1)
        v_all = jnp.concatenate([c_g_v, g_v], axis=1)
        y_c = blocked_attention(axial_rope(g_q), k_all, v_all)
        new_ctx = None
    ys = jnp.stack([y_a, y_b, y_c], axis=2)
    gates = jax.nn.sigmoid(gate_raw.reshape(b, L, N_BRANCH, D_MODEL))
    merged = jnp.sum(gates * jnp.einsum('blie,ied->blid', ys, w_branch), axis=2)
    return merged @ w_out, new_ctx


def peer(h, wq, keys, u_tab, v_tab):
    b, L, D = h.shape
    T = b * L
    t = h.reshape(T, D)
    q = (t @ wq).reshape(T, PEER_HEADS, 2, PEER_QDIM // 2)
    s = jnp.einsum('thpd,hpkd->thpk', q, keys).astype(jnp.float32)
    sv, si = lax.top_k(s, PEER_TOPK)
    cand = (sv[:, :, 0, :, None] + sv[:, :, 1, None, :]).reshape(T, PEER_HEADS, PEER_TOPK * PEER_TOPK)
    cv, ci = lax.top_k(cand, PEER_TOPK)
    e = (jnp.take_along_axis(si[:, :, 0], ci // PEER_TOPK, axis=-1) * PEER_KEYS
         + jnp.take_along_axis(si[:, :, 1], ci % PEER_TOPK, axis=-1))
    g = jax.nn.softmax(cv, axis=-1).astype(h.dtype)
    nb = T // PEER_TOK_BLOCK
    E = PEER_HEADS * PEER_TOPK

    def expert_block(args):
        tb, eb, gb = args
        act = jax.nn.gelu(jnp.einsum('td,ted->te', tb, u_tab[eb])) * gb
        return jnp.einsum('te,ted->td', act, v_tab[eb])

    out = lax.map(expert_block, (t.reshape(nb, PEER_TOK_BLOCK, D),
                                 e.reshape(nb, PEER_TOK_BLOCK, E),
                                 g.reshape(nb, PEER_TOK_BLOCK, E)))
    return out.reshape(b, L, D)


def setup_inputs(seed: int = 0) -> dict:
    key = jax.random.key(seed)
    ks = jax.random.split(key, 32)
    D = D_MODEL

    def nrm(k, shape, s):
        return jax.random.normal(k, shape, jnp.float32) * s

    a_log = jnp.log(jax.random.uniform(ks[14], (DEPTH, 2, SSD_HEADS), jnp.float32, 1.0, 16.0))
    dt0 = jnp.exp(jax.random.uniform(ks[15], (DEPTH, 2, SSD_HEADS), jnp.float32, math.log(1e-3), math.log(1e-1)))
    dt_bias = dt0 + jnp.log(-jnp.expm1(-dt0))
    return {
        'x_prompt': nrm(ks[0], (BATCH, SEQ, D), 1.0),
        'x_sample': nrm(ks[1], (DEC_BATCH, DEC_SEQ, D), 1.0),
        'cache_na_k': nrm(ks[2], (DEC_BATCH, DEPTH, PAST_LEN, NA_HEADS, HEAD_DIM), 1.0),
        'cache_na_v': nrm(ks[3], (DEC_BATCH, DEPTH, PAST_LEN, NA_HEADS, HEAD_DIM), 1.0),
        'cache_gqa_k': nrm(ks[4], (DEC_BATCH, DEPTH, PAST_LEN, GQA_KV_HEADS, HEAD_DIM), 1.0),
        'cache_gqa_v': nrm(ks[5], (DEC_BATCH, DEPTH, PAST_LEN, GQA_KV_HEADS, HEAD_DIM), 1.0),
        'state_ssd': nrm(ks[6], (DEC_BATCH, DEPTH, 2, SSD_HEADS, SSD_HEAD_DIM, SSD_STATE), 0.1),
        'c': nrm(ks[7], (DEC_BATCH, D), 1.0),
        'c_ctx': nrm(ks[8], (D,), 1.0),
        'w_mod': nrm(ks[9], (DEPTH, D, 6 * D), 0.5 * D ** -0.5),
        'b_mod': nrm(ks[10], (DEPTH, 6 * D), 0.01),
        'w_in': nrm(ks[11], (DEPTH, D, IN_COLS), D ** -0.5),
        'conv_w': nrm(ks[12], (DEPTH, CONV_W, CONV_CH), CONV_W ** -0.5),
        'conv_b': nrm(ks[13], (DEPTH, CONV_CH), 0.01),
        'ssd_a_log': a_log,
        'ssd_dt_bias': dt_bias,
        'ssd_d': 1.0 + nrm(ks[16], (DEPTH, SSD_HEADS), 0.1),
        'ssd_norm_g': 1.0 + nrm(ks[17], (DEPTH, SSD_INNER), 0.02),
        'na_rpb': nrm(ks[18], (DEPTH, NA_HEADS, 2 * NA_ROWS - 1, 2 * NA_COLS - 1), 0.02),
        'gqa_q_norm': 1.0 + nrm(ks[19], (DEPTH, HEAD_DIM), 0.02),
        'gqa_k_norm': 1.0 + nrm(ks[20], (DEPTH, HEAD_DIM), 0.02),
        'w_branch': nrm(ks[21], (DEPTH, N_BRANCH, BRANCH_W, D), BRANCH_W ** -0.5),
        'w_out': nrm(ks[22], (DEPTH, D, D), DN_BETA * D ** -0.5),
        'ln1_g': 1.0 + nrm(ks[23], (DEPTH, D), 0.02),
        'ln1_b': nrm(ks[24], (DEPTH, D), 0.01),
        'ln2_g': 1.0 + nrm(ks[25], (DEPTH, D), 0.02),
        'ln2_b': nrm(ks[26], (DEPTH, D), 0.01),
        'peer_wq': nrm(ks[27], (DEPTH, D, PEER_HEADS * PEER_QDIM), D ** -0.5),
        'peer_keys': nrm(ks[28], (DEPTH, PEER_HEADS, 2, PEER_KEYS, PEER_QDIM // 2), (PEER_QDIM // 2) ** -0.5),
        'peer_u': nrm(ks[29], (DEPTH, PEER_EXPERTS, D), D ** -0.5),
        'peer_v': nrm(ks[30], (DEPTH, PEER_EXPERTS, D), DN_BETA * PEER_HEADS ** -0.5),
    }


def reference(x_prompt, x_sample, cache_na_k, cache_na_v, cache_gqa_k, cache_gqa_v, state_ssd, c, c_ctx,
              w_mod, b_mod, w_in, conv_w, conv_b, ssd_a_log, ssd_dt_bias, ssd_d, ssd_norm_g, na_rpb,
              gqa_q_norm, gqa_k_norm, w_branch, w_out, ln1_g, ln1_b, ln2_g, ln2_b,
              peer_wq, peer_keys, peer_u, peer_v):
    def layer(x, l, cond, ctx):
        sh1, sc1, g1, sh2, sc2, g2 = adaln(cond, w_mod[l], b_mod[l])
        y, new_ctx = token_mixer(modulate(x, sh1, sc1), w_in[l], conv_w[l], conv_b[l], ssd_a_log[l],
                                 ssd_dt_bias[l], ssd_d[l], ssd_norm_g[l], na_rpb[l], gqa_q_norm[l],
                                 gqa_k_norm[l], w_branch[l], w_out[l], ctx)
        x = layer_norm(DN_ALPHA * x + g1 * y, ln1_g[l], ln1_b[l])
        f = peer(modulate(x, sh2, sc2), peer_wq[l], peer_keys[l], peer_u[l], peer_v[l])
        x = layer_norm(DN_ALPHA * x + g2 * f, ln2_g[l], ln2_b[l])
        return x, new_ctx

    xp = x_prompt
    per_layer = []
    for l in range(DEPTH):
        xp, ctx_l = layer(xp, l, c_ctx, None)
        per_layer.append(ctx_l)
    new_cache_na_k = jnp.stack([t[0] for t in per_layer], axis=1)
    new_cache_na_v = jnp.stack([t[1] for t in per_layer], axis=1)
    new_cache_gqa_k = jnp.stack([t[2] for t in per_layer], axis=1)
    new_cache_gqa_v = jnp.stack([t[3] for t in per_layer], axis=1)
    new_state_ssd = jnp.stack([t[4] for t in per_layer], axis=1)

    xs = x_sample
    for l in range(DEPTH):
        ctx_l = (cache_na_k[:, l], cache_na_v[:, l], cache_gqa_k[:, l], cache_gqa_v[:, l], state_ssd[:, l])
        xs, _ = layer(xs, l, c, ctx_l)

    return (xp, xs, new_cache_na_k, new_cache_na_v, new_cache_gqa_k, new_cache_gqa_v, new_state_ssd)
```

```python
import functools
import math

import jax
import jax.numpy as jnp
import numpy as np
from jax import lax
from jax.experimental import pallas as pl
from jax.experimental.pallas import tpu as pltpu

D_MODEL = 1024
BATCH = 16
SEQ = 256
DEPTH = 4
DEC_BATCH = 4
DEC_SEQ = 1024
PAST_LEN = 256
GRID_W = 64
HEAD_DIM = 64
BRANCH_W = 512
N_BRANCH = 3
SSD_HEADS = 8
SSD_HEAD_DIM = 64
SSD_INNER = SSD_HEADS * SSD_HEAD_DIM
SSD_GROUPS = 2
SSD_STATE = 64
SSD_BC = SSD_GROUPS * SSD_STATE
SSD_CHUNK = 128
CONV_W = 5
CONV_CH = SSD_INNER + 2 * SSD_BC
NA_HEADS = 8
NA_ROWS = 8
NA_COLS = 16
NA_SLAB = 2 * NA_COLS
GQA_HEADS = 8
GQA_KV_HEADS = 2
ROPE_THETA = 10000.0
Q_BLOCK = 128
PEER_HEADS = 8
PEER_KEYS = 128
PEER_EXPERTS = PEER_KEYS * PEER_KEYS
PEER_QDIM = 256
PEER_TOPK = 16
PEER_TOK_BLOCK = 128
DN_ALPHA = (2 * DEPTH) ** 0.25
EPS = 1e-6
IN_SIZES = (SSD_INNER, SSD_INNER, SSD_BC, SSD_BC, 2 * SSD_HEADS,
            NA_HEADS * HEAD_DIM, NA_HEADS * HEAD_DIM, NA_HEADS * HEAD_DIM,
            GQA_HEADS * HEAD_DIM, GQA_KV_HEADS * HEAD_DIM, GQA_KV_HEADS * HEAD_DIM,
            N_BRANCH * D_MODEL)


def _mm_kernel(x_ref, w_ref, o_ref):
    o_ref[...] = jnp.dot(x_ref[...].astype(jnp.bfloat16), w_ref[...].astype(jnp.bfloat16),
                         preferred_element_type=jnp.float32)


def pmatmul(x, w, tm=512, tn=None):
    m, k = x.shape
    n = w.shape[1]
    if tn is None:
        tn = min(n, 1024)
    return pl.pallas_call(
        _mm_kernel,
        grid=(n // tn, m // tm),
        in_specs=[pl.BlockSpec((tm, k), lambda j, i: (i, 0)),
                  pl.BlockSpec((k, tn), lambda j, i: (0, j))],
        out_specs=pl.BlockSpec((tm, tn), lambda j, i: (i, j)),
        out_shape=jax.ShapeDtypeStruct((m, n), jnp.float32),
    )(x, w)


def layer_norm(x, g=None, b=None):
    xf = x.astype(jnp.float32)
    mu = jnp.mean(xf, -1, keepdims=True)
    var = jnp.mean(jnp.square(xf - mu), -1, keepdims=True)
    y = ((xf - mu) * lax.rsqrt(var + EPS)).astype(x.dtype)
    if g is not None:
        y = y * g + b
    return y


def rms_norm(x, g):
    xf = x.astype(jnp.float32)
    y = xf * lax.rsqrt(jnp.mean(xf * xf, -1, keepdims=True) + EPS)
    return y.astype(x.dtype) * g


def adaln(cond, w, b):
    m = jax.nn.silu(cond) @ w + b
    return jnp.split(m[..., None, :], 6, axis=-1)


def modulate(x, shift, scale):
    return layer_norm(x) * (1 + scale) + shift


def dwconv(x, w, b):
    y = lax.conv_general_dilated(x, w[:, None, :].astype(x.dtype), window_strides=(1,),
                                 padding=[(CONV_W // 2, CONV_W // 2)],
                                 dimension_numbers=('NWC', 'WIO', 'NWC'),
                                 feature_group_count=x.shape[-1])
    return jax.nn.silu(y + b)


def segsum(a):
    cs = jnp.cumsum(a, axis=-1)
    diff = cs[..., :, None] - cs[..., None, :]
    n = a.shape[-1]
    return jnp.where(jnp.tril(jnp.ones((n, n), bool)), diff, -jnp.inf)


def ssd_scan(x, dt, A, B, C, h0):
    b, L, H, P = x.shape
    N = B.shape[-1]
    nc = L // SSD_CHUNK
    Q = SSD_CHUNK
    xd = (x * dt[..., None]).reshape(b, nc, Q, H, P)
    a = (dt * A).reshape(b, nc, Q, H).transpose(0, 3, 1, 2)
    Bc = B.reshape(b, nc, Q, H, N)
    Cc = C.reshape(b, nc, Q, H, N)
    a_cs = jnp.cumsum(a, axis=-1)
    Lmat = jnp.exp(segsum(a))
    cb = jnp.einsum('bclhn,bcshn->bhcls', Cc, Bc)
    y_diag = jnp.einsum('bhcls,bcshp->bclhp', cb * Lmat, xd)
    decay_states = jnp.exp(a_cs[..., -1:] - a_cs)
    states = jnp.einsum('bclhn,bhcl,bclhp->bchpn', Bc, decay_states, xd)
    states = jnp.concatenate([h0[:, None], states], axis=1)
    chunk_a = jnp.pad(a_cs[..., -1], ((0, 0), (0, 0), (1, 0)))
    decay_chunk = jnp.exp(segsum(chunk_a))
    new_states = jnp.einsum('bhzc,bchpn->bzhpn', decay_chunk, states)
    states, final = new_states[:, :-1], new_states[:, -1]
    y_off = jnp.einsum('bclhn,bchpn,bhcl->bclhp', Cc, states, jnp.exp(a_cs))
    return (y_diag + y_off).reshape(b, L, H, P), final


def ssd_mixer(z, xbc, dt_raw, a_log, dt_bias, d_skip, norm_g, h0):
    b, L, _ = z.shape
    xs, Bm, Cm = jnp.split(xbc.astype(jnp.float32), [SSD_INNER, SSD_INNER + SSD_BC], axis=-1)
    xh = xs.reshape(b, L, SSD_HEADS, SSD_HEAD_DIM)
    rep = SSD_HEADS // SSD_GROUPS
    Bh = jnp.repeat(Bm.reshape(b, L, SSD_GROUPS, SSD_STATE), rep, axis=2)
    Ch = jnp.repeat(Cm.reshape(b, L, SSD_GROUPS, SSD_STATE), rep, axis=2)
    dt = jax.nn.softplus(dt_raw.astype(jnp.float32).reshape(b, L, 2, SSD_HEADS) + dt_bias.astype(jnp.float32))
    A = -jnp.exp(a_log.astype(jnp.float32))
    h0 = h0.astype(jnp.float32)
    y_f, s_f = ssd_scan(xh, dt[:, :, 0], A[0], Bh, Ch, h0[:, 0])
    fl = lambda t: jnp.flip(t, axis=1)
    y_b, s_b = ssd_scan(fl(xh), fl(dt[:, :, 1]), A[1], fl(Bh), fl(Ch), h0[:, 1])
    y = y_f + fl(y_b) + d_skip.astype(jnp.float32)[:, None] * xh
    y = y.reshape(b, L, SSD_INNER) * jax.nn.silu(z.astype(jnp.float32))
    y = rms_norm(y, norm_g)
    return y.astype(z.dtype), jnp.stack([s_f, s_b], axis=1).astype(z.dtype)


def blocked_attention(q, k, v):
    b, Lq, H, d = q.shape
    kvh = k.shape[2]
    g = H // kvh
    nb = Lq // Q_BLOCK
    qb = q.reshape(b, nb, Q_BLOCK, kvh, g, d).transpose(1, 0, 2, 3, 4, 5)
    scale = d ** -0.5

    def one_block(qi):
        s = jnp.einsum('bqkgd,bskd->bkgqs', qi, k).astype(jnp.float32) * scale
        p = jax.nn.softmax(s, axis=-1).astype(v.dtype)
        return jnp.einsum('bkgqs,bskd->bqkgd', p, v)

    o = lax.map(one_block, qb)
    return o.transpose(1, 0, 2, 3, 4, 5).reshape(b, Lq, H * d)


def axial_rope(x):
    L = x.shape[1]
    t = jnp.arange(L)
    row = (t // GRID_W).astype(jnp.float32)
    col = (t % GRID_W).astype(jnp.float32)
    half = x.shape[-1] // 2
    inv = 1.0 / (ROPE_THETA ** (jnp.arange(0, half, 2, dtype=jnp.float32) / half))

    def rot(xh, pos):
        ang = pos[:, None] * inv
        cos = jnp.cos(ang)[None, :, None, :]
        sin = jnp.sin(ang)[None, :, None, :]
        x1, x2 = jnp.split(xh, 2, axis=-1)
        return jnp.concatenate([x1 * cos - x2 * sin, x2 * cos + x1 * sin], axis=-1)

    xr, xc = jnp.split(x.astype(jnp.float32), 2, axis=-1)
    return jnp.concatenate([rot(xr, row), rot(xc, col)], axis=-1).astype(x.dtype)


def neighbourhood_attention(q, k, v, ctx_k, ctx_v, rpb):
    b, L, H, d = q.shape
    rows = L // GRID_W
    kr = min(NA_ROWS, rows)
    ncb = GRID_W // NA_COLS
    r = jnp.arange(rows)
    key_rows = jnp.clip(r - kr // 2, 0, rows - kr)[:, None] + jnp.arange(kr)
    slab0 = jnp.clip(jnp.arange(ncb) * NA_COLS - NA_COLS // 2, 0, GRID_W - NA_SLAB)
    key_cols = slab0[:, None] + jnp.arange(NA_SLAB)
    qcol = jnp.arange(ncb)[:, None] * NA_COLS + jnp.arange(NA_COLS)
    qstart = jnp.clip(qcol - NA_COLS // 2, 0, GRID_W - NA_COLS)
    kc = key_cols[:, None, :]
    valid = (kc >= qstart[..., None]) & (kc < qstart[..., None] + NA_COLS)
    dr_idx = key_rows - r[:, None] + NA_ROWS - 1
    dc_idx = jnp.clip(kc - qcol[..., None] + NA_COLS - 1, 0, 2 * NA_COLS - 2)
    bias = rpb[:, dr_idx[:, None, None, :, None], dc_idx[None, :, :, None, :]].astype(jnp.float32)
    kg = k.reshape(b, rows, GRID_W, H, d)
    vg = v.reshape(b, rows, GRID_W, H, d)
    ri = key_rows[:, None, :, None]
    ci = key_cols[None, :, None, :]
    kb = kg[:, ri, ci]
    vb = vg[:, ri, ci].reshape(b, rows, ncb, kr * NA_SLAB, H, d)
    qb = q.reshape(b, rows, ncb, NA_COLS, H, d)
    scale = d ** -0.5
    s_loc = jnp.einsum('brcqhd,brcijhd->bhrcqij', qb, kb).astype(jnp.float32) * scale + bias[None]
    s_loc = jnp.where(valid[:, :, None, :], s_loc, -jnp.inf)
    n_loc = kr * NA_SLAB
    s_loc = s_loc.reshape(b, H, rows, ncb, NA_COLS, n_loc)
    s_ctx = jnp.einsum('brcqhd,bshd->bhrcqs', qb, ctx_k).astype(jnp.float32) * scale
    p = jax.nn.softmax(jnp.concatenate([s_loc, s_ctx], axis=-1), axis=-1).astype(v.dtype)
    o = (jnp.einsum('bhrcqn,brcnhd->brcqhd', p[..., :n_loc], vb)
         + jnp.einsum('bhrcqs,bshd->brcqhd', p[..., n_loc:], ctx_v))
    return o.reshape(b, L, H * d)


def token_mixer(h, w_in, conv_w, conv_b, a_log, dt_bias, d_skip, ssd_g, rpb, qn_g, kn_g, w_branch, w_out, ctx):
    b, L, _ = h.shape
    p = h @ w_in
    offs = np.cumsum(IN_SIZES)[:-1].tolist()
    z, xs, Bm, Cm, dt_raw, na_q, na_k, na_v, g_q, g_k, g_v, gate_raw = jnp.split(p, offs, axis=-1)
    xbc = dwconv(jnp.concatenate([xs, Bm, Cm], axis=-1), conv_w, conv_b)
    na_q = na_q.reshape(b, L, NA_HEADS, HEAD_DIM)
    na_k = na_k.reshape(b, L, NA_HEADS, HEAD_DIM)
    na_v = na_v.reshape(b, L, NA_HEADS, HEAD_DIM)
    g_q = rms_norm(g_q.reshape(b, L, GQA_HEADS, HEAD_DIM), qn_g)
    g_k = rms_norm(g_k.reshape(b, L, GQA_KV_HEADS, HEAD_DIM), kn_g)
    g_v = g_v.reshape(b, L, GQA_KV_HEADS, HEAD_DIM)
    if ctx is None:
        h0 = jnp.zeros((b, 2, SSD_HEADS, SSD_HEAD_DIM, SSD_STATE), h.dtype)
        y_a, ssd_state = ssd_mixer(z, xbc, dt_raw, a_log, dt_bias, d_skip, ssd_g, h0)
        y_b = blocked_attention(na_q, na_k, na_v)
        y_c = blocked_attention(g_q, g_k, g_v)
        new_ctx = (na_k, na_v, g_k, g_v, ssd_state)
    else:
        c_na_k, c_na_v, c_g_k, c_g_v, c_ssd = ctx
        y_a, _ = ssd_mixer(z, xbc, dt_raw, a_log, dt_bias, d_skip, ssd_g, c_ssd)
        y_b = neighbourhood_attention(na_q, na_k, na_v, c_na_k, c_na_v, rpb)
        k_all = jnp.concatenate([c_g_k, axial_rope(g_k)], axis=1)
        v_all = jnp.concatenate([c_g_v, g_v], axis=1)
        y_c = blocked_attention(axial_rope(g_q), k_all, v_all)
        new_ctx = None
    ys = jnp.stack([y_a, y_b, y_c], axis=2)
    gates = jax.nn.sigmoid(gate_raw.reshape(b, L, N_BRANCH, D_MODEL))
    merged = jnp.sum(gates * jnp.einsum('blie,ied->blid', ys, w_branch), axis=2)
    return pmatmul(merged.reshape(b * L, D_MODEL), w_out).reshape(b, L, D_MODEL), new_ctx


def peer(h, wq, keys, u_tab, v_tab):
    b, L, D = h.shape
    T = b * L
    t = h.reshape(T, D)
    q = pmatmul(t, wq).reshape(T, PEER_HEADS, 2, PEER_QDIM // 2)
    s = jnp.einsum('thpd,hpkd->thpk', q, keys).astype(jnp.float32)
    sv, si = lax.top_k(s, PEER_TOPK)
    cand = (sv[:, :, 0, :, None] + sv[:, :, 1, None, :]).reshape(T, PEER_HEADS, PEER_TOPK * PEER_TOPK)
    cv, ci = lax.top_k(cand, PEER_TOPK)
    e = (jnp.take_along_axis(si[:, :, 0], ci // PEER_TOPK, axis=-1) * PEER_KEYS
         + jnp.take_along_axis(si[:, :, 1], ci % PEER_TOPK, axis=-1))
    g = jax.nn.softmax(cv, axis=-1).astype(h.dtype)
    nb = T // PEER_TOK_BLOCK
    E = PEER_HEADS * PEER_TOPK

    def expert_block(args):
        tb, eb, gb = args
        act = jax.nn.gelu(jnp.einsum('td,ted->te', tb, u_tab[eb])) * gb
        return jnp.einsum('te,ted->td', act, v_tab[eb])

    out = lax.map(expert_block, (t.reshape(nb, PEER_TOK_BLOCK, D),
                                 e.reshape(nb, PEER_TOK_BLOCK, E),
                                 g.reshape(nb, PEER_TOK_BLOCK, E)))
    return out.reshape(b, L, D)


def kernel(x_prompt, x_sample, cache_na_k, cache_na_v, cache_gqa_k, cache_gqa_v, state_ssd, c, c_ctx,
           w_mod, b_mod, w_in, conv_w, conv_b, ssd_a_log, ssd_dt_bias, ssd_d, ssd_norm_g, na_rpb,
           gqa_q_norm, gqa_k_norm, w_branch, w_out, ln1_g, ln1_b, ln2_g, ln2_b,
           peer_wq, peer_keys, peer_u, peer_v):
    def layer(x, l, cond, ctx):
        sh1, sc1, g1, sh2, sc2, g2 = adaln(cond, w_mod[l], b_mod[l])
        y, new_ctx = token_mixer(modulate(x, sh1, sc1), w_in[l], conv_w[l], conv_b[l], ssd_a_log[l],
                                 ssd_dt_bias[l], ssd_d[l], ssd_norm_g[l], na_rpb[l], gqa_q_norm[l],
                                 gqa_k_norm[l], w_branch[l], w_out[l], ctx)
        x = layer_norm(DN_ALPHA * x + g1 * y, ln1_g[l], ln1_b[l])
        f = peer(modulate(x, sh2, sc2), peer_wq[l], peer_keys[l], peer_u[l], peer_v[l])
        x = layer_norm(DN_ALPHA * x + g2 * f, ln2_g[l], ln2_b[l])
        return x, new_ctx

    xp = x_prompt
    per_layer = []
    for l in range(DEPTH):
        xp, ctx_l = layer(xp, l, c_ctx, None)
        per_layer.append(ctx_l)
    new_cache_na_k = jnp.stack([t[0] for t in per_layer], axis=1)
    new_cache_na_v = jnp.stack([t[1] for t in per_layer], axis=1)
    new_cache_gqa_k = jnp.stack([t[2] for t in per_layer], axis=1)
    new_cache_gqa_v = jnp.stack([t[3] for t in per_layer], axis=1)
    new_state_ssd = jnp.stack([t[4] for t in per_layer], axis=1)

    xs = x_sample
    for l in range(DEPTH):
        ctx_l = (cache_na_k[:, l], cache_na_v[:, l], cache_gqa_k[:, l], cache_gqa_v[:, l], state_ssd[:, l])
        xs, _ = layer(xs, l, c, ctx_l)

    return (xp, xs, new_cache_na_k, new_cache_na_v, new_cache_gqa_k, new_cache_gqa_v, new_state_ssd)
```

```python
import functools
import math

import jax
import jax.numpy as jnp
import numpy as np
from jax import lax
from jax.experimental import pallas as pl
from jax.experimental.pallas import tpu as pltpu

D_MODEL = 1024
BATCH = 16
SEQ = 256
DEPTH = 4
DEC_BATCH = 4
DEC_SEQ = 1024
PAST_LEN = 256
GRID_W = 64
HEAD_DIM = 64
BRANCH_W = 512
N_BRANCH = 3
SSD_HEADS = 8
SSD_HEAD_DIM = 64
SSD_INNER = SSD_HEADS * SSD_HEAD_DIM
SSD_GROUPS = 2
SSD_STATE = 64
SSD_BC = SSD_GROUPS * SSD_STATE
SSD_CHUNK = 128
CONV_W = 5
CONV_CH = SSD_INNER + 2 * SSD_BC
NA_HEADS = 8
NA_ROWS = 8
NA_COLS = 16
NA_SLAB = 2 * NA_COLS
GQA_HEADS = 8
GQA_KV_HEADS = 2
ROPE_THETA = 10000.0
Q_BLOCK = 128
PEER_HEADS = 8
PEER_KEYS = 128
PEER_EXPERTS = PEER_KEYS * PEER_KEYS
PEER_QDIM = 256
PEER_TOPK = 16
PEER_TOK_BLOCK = 128
DN_ALPHA = (2 * DEPTH) ** 0.25
EPS = 1e-6
IN_SIZES = (SSD_INNER, SSD_INNER, SSD_BC, SSD_BC, 2 * SSD_HEADS,
            NA_HEADS * HEAD_DIM, NA_HEADS * HEAD_DIM, NA_HEADS * HEAD_DIM,
            GQA_HEADS * HEAD_DIM, GQA_KV_HEADS * HEAD_DIM, GQA_KV_HEADS * HEAD_DIM,
            N_BRANCH * D_MODEL)


def _mm_kernel(x_ref, w_ref, o_ref):
    o_ref[...] = jnp.dot(x_ref[...].astype(jnp.bfloat16), w_ref[...].astype(jnp.bfloat16),
                         preferred_element_type=jnp.float32)


def pmatmul(x, w, tm=512, tn=None):
    m, k = x.shape
    n = w.shape[1]
    if tn is None:
        tn = min(n, 1024)
    return pl.pallas_call(
        _mm_kernel,
        grid=(n // tn, m // tm),
        in_specs=[pl.BlockSpec((tm, k), lambda j, i: (i, 0)),
                  pl.BlockSpec((k, tn), lambda j, i: (0, j))],
        out_specs=pl.BlockSpec((tm, tn), lambda j, i: (i, j)),
        out_shape=jax.ShapeDtypeStruct((m, n), jnp.float32),
    )(x, w)


G3_PITCH = 136
VMEM_LIMIT_BYTES = 56 * 1024 * 1024


def _gelu_tanh(x):
    return 0.5 * x * (1.0 + jnp.tanh(0.7978845608028654 * (x + 0.044715 * (x * x * x))))


def _peer_dense_kernel(x_ref, e_ref, g_ref, ut_ref, v_ref, o_ref, g3_ref, acc_ref, *, t_blk, te):
    j = pl.program_id(1)
    nk = PEER_KEYS
    n_i = te // nk

    @pl.when(j == 0)
    def _build():
        acc_ref[...] = jnp.zeros_like(acc_ref)
        sub_iota = lax.broadcasted_iota(jnp.int32, (nk, nk), 0)

        def body(t, carry):
            e_row = e_ref[pl.ds(t, 1), :]
            g_row = g_ref[pl.ds(t, 1), :]
            i_row = lax.shift_right_logical(e_row, 7)
            j_row = lax.bitwise_and(e_row, nk - 1)
            g_hi = g_row.astype(jnp.bfloat16).astype(jnp.float32)
            g_lo = g_row - g_hi
            sel_i = sub_iota == i_row
            a_hi = jnp.where(sel_i, g_hi, 0.0).astype(jnp.bfloat16)
            a_lo = jnp.where(sel_i, g_lo, 0.0).astype(jnp.bfloat16)
            b_t = jnp.where(sub_iota == j_row, 1.0, 0.0).astype(jnp.bfloat16)
            lhs = jnp.concatenate([a_hi, a_lo], axis=1)
            rhs = jnp.concatenate([b_t, b_t], axis=1)
            gt = lax.dot_general(lhs, rhs, (((1,), (1,)), ((), ())), preferred_element_type=jnp.float32)
            g3_ref[pl.ds(pl.multiple_of(t * G3_PITCH, 8), nk), :] = gt
            return carry

        lax.fori_loop(0, t_blk, body, 0)

    s = jnp.dot(x_ref[...], ut_ref[...], preferred_element_type=jnp.float32)
    parts = []
    for k in range(n_i):
        gi = g3_ref[pl.ds(j * n_i + k, t_blk, stride=G3_PITCH), :]
        parts.append((_gelu_tanh(s[:, k * nk:(k + 1) * nk]) * gi).astype(jnp.bfloat16))
    a = jnp.concatenate(parts, axis=1)
    acc_ref[...] += jnp.dot(a, v_ref[...], preferred_element_type=jnp.float32)

    @pl.when(j == pl.num_programs(1) - 1)
    def _fin():
        o_ref[...] = acc_ref[...]


def peer_dense(x_bf16, e, g, ut_bf16, v_bf16, t_blk=512, te=512):
    T, D = x_bf16.shape
    NE = v_bf16.shape[0]
    kern = functools.partial(_peer_dense_kernel, t_blk=t_blk, te=te)
    return pl.pallas_call(
        kern,
        grid=(T // t_blk, NE // te),
        in_specs=[pl.BlockSpec((t_blk, D), lambda i, j: (i, 0)),
                  pl.BlockSpec((t_blk, 128), lambda i, j: (i, 0)),
                  pl.BlockSpec((t_blk, 128), lambda i, j: (i, 0)),
                  pl.BlockSpec((D, te), lambda i, j: (0, j)),
                  pl.BlockSpec((te, D), lambda i, j: (j, 0))],
        out_specs=pl.BlockSpec((t_blk, D), lambda i, j: (i, 0)),
        out_shape=jax.ShapeDtypeStruct((T, D), jnp.float32),
        scratch_shapes=[pltpu.VMEM((t_blk * G3_PITCH, PEER_KEYS), jnp.float32),
                        pltpu.VMEM((t_blk, D), jnp.float32)],
        compiler_params=pltpu.CompilerParams(dimension_semantics=("arbitrary", "arbitrary"),
                                             vmem_limit_bytes=VMEM_LIMIT_BYTES),
        name="peer_dense",
    )(x_bf16, e, g, ut_bf16, v_bf16)


def layer_norm(x, g=None, b=None):
    xf = x.astype(jnp.float32)
    mu = jnp.mean(xf, -1, keepdims=True)
    var = jnp.mean(jnp.square(xf - mu), -1, keepdims=True)
    y = ((xf - mu) * lax.rsqrt(var + EPS)).astype(x.dtype)
    if g is not None:
        y = y * g + b
    return y


def rms_norm(x, g):
    xf = x.astype(jnp.float32)
    y = xf * lax.rsqrt(jnp.mean(xf * xf, -1, keepdims=True) + EPS)
    return y.astype(x.dtype) * g


def adaln(cond, w, b):
    m = jax.nn.silu(cond) @ w + b
    return jnp.split(m[..., None, :], 6, axis=-1)


def modulate(x, shift, scale):
    return layer_norm(x) * (1 + scale) + shift


def dwconv(x, w, b):
    y = lax.conv_general_dilated(x, w[:, None, :].astype(x.dtype), window_strides=(1,),
                                 padding=[(CONV_W // 2, CONV_W // 2)],
                                 dimension_numbers=('NWC', 'WIO', 'NWC'),
                                 feature_group_count=x.shape[-1])
    return jax.nn.silu(y + b)


def segsum(a):
    cs = jnp.cumsum(a, axis=-1)
    diff = cs[..., :, None] - cs[..., None, :]
    n = a.shape[-1]
    return jnp.where(jnp.tril(jnp.ones((n, n), bool)), diff, -jnp.inf)


def ssd_scan(x, dt, A, B, C, h0):
    b, L, H, P = x.shape
    N = B.shape[-1]
    nc = L // SSD_CHUNK
    Q = SSD_CHUNK
    xd = (x * dt[..., None]).reshape(b, nc, Q, H, P)
    a = (dt * A).reshape(b, nc, Q, H).transpose(0, 3, 1, 2)
    Bc = B.reshape(b, nc, Q, H, N)
    Cc = C.reshape(b, nc, Q, H, N)
    a_cs = jnp.cumsum(a, axis=-1)
    Lmat = jnp.exp(segsum(a))
    cb = jnp.einsum('bclhn,bcshn->bhcls', Cc, Bc)
    y_diag = jnp.einsum('bhcls,bcshp->bclhp', cb * Lmat, xd)
    decay_states = jnp.exp(a_cs[..., -1:] - a_cs)
    states = jnp.einsum('bclhn,bhcl,bclhp->bchpn', Bc, decay_states, xd)
    states = jnp.concatenate([h0[:, None], states], axis=1)
    chunk_a = jnp.pad(a_cs[..., -1], ((0, 0), (0, 0), (1, 0)))
    decay_chunk = jnp.exp(segsum(chunk_a))
    new_states = jnp.einsum('bhzc,bchpn->bzhpn', decay_chunk, states)
    states, final = new_states[:, :-1], new_states[:, -1]
    y_off = jnp.einsum('bclhn,bchpn,bhcl->bclhp', Cc, states, jnp.exp(a_cs))
    return (y_diag + y_off).reshape(b, L, H, P), final


def ssd_mixer(z, xbc, dt_raw, a_log, dt_bias, d_skip, norm_g, h0):
    b, L, _ = z.shape
    xs, Bm, Cm = jnp.split(xbc.astype(jnp.float32), [SSD_INNER, SSD_INNER + SSD_BC], axis=-1)
    xh = xs.reshape(b, L, SSD_HEADS, SSD_HEAD_DIM)
    rep = SSD_HEADS // SSD_GROUPS
    Bh = jnp.repeat(Bm.reshape(b, L, SSD_GROUPS, SSD_STATE), rep, axis=2)
    Ch = jnp.repeat(Cm.reshape(b, L, SSD_GROUPS, SSD_STATE), rep, axis=2)
    dt = jax.nn.softplus(dt_raw.astype(jnp.float32).reshape(b, L, 2, SSD_HEADS) + dt_bias.astype(jnp.float32))
    A = -jnp.exp(a_log.astype(jnp.float32))
    h0 = h0.astype(jnp.float32)
    y_f, s_f = ssd_scan(xh, dt[:, :, 0], A[0], Bh, Ch, h0[:, 0])
    fl = lambda t: jnp.flip(t, axis=1)
    y_b, s_b = ssd_scan(fl(xh), fl(dt[:, :, 1]), A[1], fl(Bh), fl(Ch), h0[:, 1])
    y = y_f + fl(y_b) + d_skip.astype(jnp.float32)[:, None] * xh
    y = y.reshape(b, L, SSD_INNER) * jax.nn.silu(z.astype(jnp.float32))
    y = rms_norm(y, norm_g)
    return y.astype(z.dtype), jnp.stack([s_f, s_b], axis=1).astype(z.dtype)


def blocked_attention(q, k, v):
    b, Lq, H, d = q.shape
    kvh = k.shape[2]
    g = H // kvh
    nb = Lq // Q_BLOCK
    qb = q.reshape(b, nb, Q_BLOCK, kvh, g, d).transpose(1, 0, 2, 3, 4, 5)
    scale = d ** -0.5

    def one_block(qi):
        s = jnp.einsum('bqkgd,bskd->bkgqs', qi, k).astype(jnp.float32) * scale
        p = jax.nn.softmax(s, axis=-1).astype(v.dtype)
        return jnp.einsum('bkgqs,bskd->bqkgd', p, v)

    o = lax.map(one_block, qb)
    return o.transpose(1, 0, 2, 3, 4, 5).reshape(b, Lq, H * d)


def axial_rope(x):
    L = x.shape[1]
    t = jnp.arange(L)
    row = (t // GRID_W).astype(jnp.float32)
    col = (t % GRID_W).astype(jnp.float32)
    half = x.shape[-1] // 2
    inv = 1.0 / (ROPE_THETA ** (jnp.arange(0, half, 2, dtype=jnp.float32) / half))

    def rot(xh, pos):
        ang = pos[:, None] * inv
        cos = jnp.cos(ang)[None, :, None, :]
        sin = jnp.sin(ang)[None, :, None, :]
        x1, x2 = jnp.split(xh, 2, axis=-1)
        return jnp.concatenate([x1 * cos - x2 * sin, x2 * cos + x1 * sin], axis=-1)

    xr, xc = jnp.split(x.astype(jnp.float32), 2, axis=-1)
    return jnp.concatenate([rot(xr, row), rot(xc, col)], axis=-1).astype(x.dtype)


def neighbourhood_attention(q, k, v, ctx_k, ctx_v, rpb):
    b, L, H, d = q.shape
    rows = L // GRID_W
    kr = min(NA_ROWS, rows)
    ncb = GRID_W // NA_COLS
    r = jnp.arange(rows)
    key_rows = jnp.clip(r - kr // 2, 0, rows - kr)[:, None] + jnp.arange(kr)
    slab0 = jnp.clip(jnp.arange(ncb) * NA_COLS - NA_COLS // 2, 0, GRID_W - NA_SLAB)
    key_cols = slab0[:, None] + jnp.arange(NA_SLAB)
    qcol = jnp.arange(ncb)[:, None] * NA_COLS + jnp.arange(NA_COLS)
    qstart = jnp.clip(qcol - NA_COLS // 2, 0, GRID_W - NA_COLS)
    kc = key_cols[:, None, :]
    valid = (kc >= qstart[..., None]) & (kc < qstart[..., None] + NA_COLS)
    dr_idx = key_rows - r[:, None] + NA_ROWS - 1
    dc_idx = jnp.clip(kc - qcol[..., None] + NA_COLS - 1, 0, 2 * NA_COLS - 2)
    bias = rpb[:, dr_idx[:, None, None, :, None], dc_idx[None, :, :, None, :]].astype(jnp.float32)
    kg = k.reshape(b, rows, GRID_W, H, d)
    vg = v.reshape(b, rows, GRID_W, H, d)
    ri = key_rows[:, None, :, None]
    ci = key_cols[None, :, None, :]
    kb = kg[:, ri, ci]
    vb = vg[:, ri, ci].reshape(b, rows, ncb, kr * NA_SLAB, H, d)
    qb = q.reshape(b, rows, ncb, NA_COLS, H, d)
    scale = d ** -0.5
    s_loc = jnp.einsum('brcqhd,brcijhd->bhrcqij', qb, kb).astype(jnp.float32) * scale + bias[None]
    s_loc = jnp.where(valid[:, :, None, :], s_loc, -jnp.inf)
    n_loc = kr * NA_SLAB
    s_loc = s_loc.reshape(b, H, rows, ncb, NA_COLS, n_loc)
    s_ctx = jnp.einsum('brcqhd,bshd->bhrcqs', qb, ctx_k).astype(jnp.float32) * scale
    p = jax.nn.softmax(jnp.concatenate([s_loc, s_ctx], axis=-1), axis=-1).astype(v.dtype)
    o = (jnp.einsum('bhrcqn,brcnhd->brcqhd', p[..., :n_loc], vb)
         + jnp.einsum('bhrcqs,bshd->brcqhd', p[..., n_loc:], ctx_v))
    return o.reshape(b, L, H * d)


def token_mixer(h, w_in, conv_w, conv_b, a_log, dt_bias, d_skip, ssd_g, rpb, qn_g, kn_g, w_branch, w_out, ctx):
    b, L, _ = h.shape
    p = h @ w_in
    offs = np.cumsum(IN_SIZES)[:-1].tolist()
    z, xs, Bm, Cm, dt_raw, na_q, na_k, na_v, g_q, g_k, g_v, gate_raw = jnp.split(p, offs, axis=-1)
    xbc = dwconv(jnp.concatenate([xs, Bm, Cm], axis=-1), conv_w, conv_b)
    na_q = na_q.reshape(b, L, NA_HEADS, HEAD_DIM)
    na_k = na_k.reshape(b, L, NA_HEADS, HEAD_DIM)
    na_v = na_v.reshape(b, L, NA_HEADS, HEAD_DIM)
    g_q = rms_norm(g_q.reshape(b, L, GQA_HEADS, HEAD_DIM), qn_g)
    g_k = rms_norm(g_k.reshape(b, L, GQA_KV_HEADS, HEAD_DIM), kn_g)
    g_v = g_v.reshape(b, L, GQA_KV_HEADS, HEAD_DIM)
    if ctx is None:
        h0 = jnp.zeros((b, 2, SSD_HEADS, SSD_HEAD_DIM, SSD_STATE), h.dtype)
        y_a, ssd_state = ssd_mixer(z, xbc, dt_raw, a_log, dt_bias, d_skip, ssd_g, h0)
        y_b = blocked_attention(na_q, na_k, na_v)
        y_c = blocked_attention(g_q, g_k, g_v)
        new_ctx = (na_k, na_v, g_k, g_v, ssd_state)
    else:
        c_na_k, c_na_v, c_g_k, c_g_v, c_ssd = ctx
        y_a, _ = ssd_mixer(z, xbc, dt_raw, a_log, dt_bias, d_skip, ssd_g, c_ssd)
        y_b = neighbourhood_attention(na_q, na_k, na_v, c_na_k, c_na_v, rpb)
        k_all = jnp.concatenate([c_g_k, axial_rope(g_k)], axis=1)
        v_all = jnp.concatenate([c_g_v, g_v], axis=1)
        y_c = blocked_attention(axial_rope(g_q), k_all, v_all)
        new_ctx = None
    ys = jnp.stack([y_a, y_b, y_c], axis=2)
    gates = jax.nn.sigmoid(gate_raw.reshape(b, L, N_BRANCH, D_MODEL))
    merged = jnp.sum(gates * jnp.einsum('blie,ied->blid', ys, w_branch), axis=2)
    return pmatmul(merged.reshape(b * L, D_MODEL), w_out).reshape(b, L, D_MODEL), new_ctx


def peer(h, wq, keys, u_tab, v_tab):
    b, L, D = h.shape
    T = b * L
    t = h.reshape(T, D)
    q = pmatmul(t, wq).reshape(T, PEER_HEADS, 2, PEER_QDIM // 2)
    s = jnp.einsum('thpd,hpkd->thpk', q, keys).astype(jnp.float32)
    sv, si = lax.top_k(s, PEER_TOPK)
    cand = (sv[:, :, 0, :, None] + sv[:, :, 1, None, :]).reshape(T, PEER_HEADS, PEER_TOPK * PEER_TOPK)
    cv, ci = lax.top_k(cand, PEER_TOPK)
    e = (jnp.take_along_axis(si[:, :, 0], ci // PEER_TOPK, axis=-1) * PEER_KEYS
         + jnp.take_along_axis(si[:, :, 1], ci % PEER_TOPK, axis=-1))
    g = jax.nn.softmax(cv, axis=-1).astype(h.dtype)
    E = PEER_HEADS * PEER_TOPK
    out = peer_dense(t.astype(jnp.bfloat16), e.reshape(T, E).astype(jnp.int32), g.reshape(T, E),
                     u_tab, v_tab)
    return out.reshape(b, L, D)


def kernel(x_prompt, x_sample, cache_na_k, cache_na_v, cache_gqa_k, cache_gqa_v, state_ssd, c, c_ctx,
           w_mod, b_mod, w_in, conv_w, conv_b, ssd_a_log, ssd_dt_bias, ssd_d, ssd_norm_g, na_rpb,
           gqa_q_norm, gqa_k_norm, w_branch, w_out, ln1_g, ln1_b, ln2_g, ln2_b,
           peer_wq, peer_keys, peer_u, peer_v):
    def layer(x, l, cond, ctx):
        sh1, sc1, g1, sh2, sc2, g2 = adaln(cond, w_mod[l], b_mod[l])
        y, new_ctx = token_mixer(modulate(x, sh1, sc1), w_in[l], conv_w[l], conv_b[l], ssd_a_log[l],
                                 ssd_dt_bias[l], ssd_d[l], ssd_norm_g[l], na_rpb[l], gqa_q_norm[l],
                                 gqa_k_norm[l], w_branch[l], w_out[l], ctx)
        x = layer_norm(DN_ALPHA * x + g1 * y, ln1_g[l], ln1_b[l])
        f = peer(modulate(x, sh2, sc2), peer_wq[l], peer_keys[l], peer_ut[l], peer_vb[l])
        x = layer_norm(DN_ALPHA * x + g2 * f, ln2_g[l], ln2_b[l])
        return x, new_ctx

    peer_ut = [peer_u[l].T.astype(jnp.bfloat16) for l in range(DEPTH)]
    peer_vb = [peer_v[l].astype(jnp.bfloat16) for l in range(DEPTH)]

    xp = x_prompt
    per_layer = []
    for l in range(DEPTH):
        xp, ctx_l = layer(xp, l, c_ctx, None)
        per_layer.append(ctx_l)
    new_cache_na_k = jnp.stack([t[0] for t in per_layer], axis=1)
    new_cache_na_v = jnp.stack([t[1] for t in per_layer], axis=1)
    new_cache_gqa_k = jnp.stack([t[2] for t in per_layer], axis=1)
    new_cache_gqa_v = jnp.stack([t[3] for t in per_layer], axis=1)
    new_state_ssd = jnp.stack([t[4] for t in per_layer], axis=1)

    xs = x_sample
    for l in range(DEPTH):
        ctx_l = (cache_na_k[:, l], cache_na_v[:, l], cache_gqa_k[:, l], cache_gqa_v[:, l], state_ssd[:, l])
        xs, _ = layer(xs, l, c, ctx_l)

    return (xp, xs, new_cache_na_k, new_cache_na_v, new_cache_gqa_k, new_cache_gqa_v, new_state_ssd)
```

```python
import functools

import jax
import jax.numpy as jnp
import numpy as np
from jax import lax
from jax.experimental import pallas as pl
from jax.experimental.pallas import tpu as pltpu

D_MODEL = 1024
BATCH = 16
SEQ = 256
DEPTH = 4
DEC_BATCH = 4
DEC_SEQ = 1024
PAST_LEN = 256
GRID_W = 64
HEAD_DIM = 64
N_BRANCH = 3
SSD_HEADS = 8
SSD_HEAD_DIM = 64
SSD_INNER = SSD_HEADS * SSD_HEAD_DIM
SSD_GROUPS = 2
SSD_STATE = 64
SSD_BC = SSD_GROUPS * SSD_STATE
SSD_CHUNK = 128
CONV_W = 5
NA_HEADS = 8
NA_ROWS = 8
NA_COLS = 16
GQA_HEADS = 8
GQA_KV_HEADS = 2
ROPE_THETA = 10000.0
PEER_HEADS = 8
PEER_KEYS = 128
PEER_TOPK = 16
DN_ALPHA = (2 * DEPTH) ** 0.25
EPS = 1e-6
IN_SIZES = (SSD_INNER, SSD_INNER, SSD_BC, SSD_BC, 2 * SSD_HEADS,
            NA_HEADS * HEAD_DIM, NA_HEADS * HEAD_DIM, NA_HEADS * HEAD_DIM,
            GQA_HEADS * HEAD_DIM, GQA_KV_HEADS * HEAD_DIM, GQA_KV_HEADS * HEAD_DIM,
            N_BRANCH * D_MODEL)

LANES = 128
VMEM_LIMIT_BYTES = 56 * 1024 * 1024
NEG_BIG = -1e30
ROW_TILE = 256
COND_ROWS = 8
DT_PAD = LANES
IN_OUT_WIDTHS = (SSD_INNER, SSD_INNER, 2 * SSD_BC, DT_PAD, 512, 512, 512, 512, 128, 128, N_BRANCH * D_MODEL)
PEER_TOK_BLOCK = 512
PEER_EXPERT_TILE = 512
G3_PITCH = 136
TOPK_TOK = LANES


def _cond_row(i, is_ctx, rows_per_step):
    if is_ctx:
        return 0
    return 1 + i // (DEC_SEQ // rows_per_step)


def _ln(x):
    mu = jnp.mean(x, axis=-1, keepdims=True)
    xc = x - mu
    var = jnp.mean(xc * xc, axis=-1, keepdims=True)
    return xc * lax.rsqrt(var + EPS)


def _silu(x):
    return x / (1.0 + jnp.exp(-x))


def _sigmoid(x):
    return 1.0 / (1.0 + jnp.exp(-x))


def _gelu_tanh(x):
    return 0.5 * x * (1.0 + jnp.tanh(0.7978845608028654 * (x + 0.044715 * (x * x * x))))


def _adaln_kernel(c_ref, w_ref, b_ref, o_ref):
    a = _silu(c_ref[...]).astype(jnp.bfloat16)
    o_ref[0] = jnp.dot(a, w_ref[0].astype(jnp.bfloat16), preferred_element_type=jnp.float32) + b_ref[0]


def adaln_table(cond, w_mod, b_mod, tn=1536):
    depth, d, n = w_mod.shape
    return pl.pallas_call(
        _adaln_kernel, grid=(depth, n // tn),
        in_specs=[pl.BlockSpec((COND_ROWS, d), lambda l, j: (0, 0)),
                  pl.BlockSpec((1, d, tn), lambda l, j: (l, 0, j)),
                  pl.BlockSpec((1, 1, tn), lambda l, j: (l, 0, j))],
        out_specs=pl.BlockSpec((1, COND_ROWS, tn), lambda l, j: (l, 0, j)),
        out_shape=jax.ShapeDtypeStruct((depth, COND_ROWS, n), jnp.float32),
        compiler_params=pltpu.CompilerParams(dimension_semantics=("arbitrary", "arbitrary"),
                                             vmem_limit_bytes=VMEM_LIMIT_BYTES),
        name="adaln_table")(cond, w_mod, b_mod.reshape(depth, 1, n))


def _in_proj_kernel(x_ref, mod_ref, w_ref, *o_refs):
    m = mod_ref[0]
    h = _ln(x_ref[...]) * (1.0 + m[:, D_MODEL:2 * D_MODEL]) + m[:, 0:D_MODEL]
    p = jnp.dot(h.astype(jnp.bfloat16), w_ref[...], preferred_element_type=jnp.float32)
    off = 0
    for o_ref, w in zip(o_refs, IN_OUT_WIDTHS):
        o_ref[...] = p[:, off:off + w]
        off += w


def in_proj(x, mod, w_cat, is_ctx):
    T, d = x.shape
    n = w_cat.shape[1]
    return pl.pallas_call(
        _in_proj_kernel, grid=(T // ROW_TILE,),
        in_specs=[pl.BlockSpec((ROW_TILE, d), lambda i: (i, 0)),
                  pl.BlockSpec((1, 1, mod.shape[2]), lambda i: (_cond_row(i, is_ctx, ROW_TILE), 0, 0)),
                  pl.BlockSpec((d, n), lambda i: (0, 0), pipeline_mode=pl.Buffered(1))],
        out_specs=[pl.BlockSpec((ROW_TILE, w), lambda i: (i, 0)) for w in IN_OUT_WIDTHS],
        out_shape=[jax.ShapeDtypeStruct((T, w), jnp.float32) for w in IN_OUT_WIDTHS],
        compiler_params=pltpu.CompilerParams(dimension_semantics=("arbitrary",), vmem_limit_bytes=VMEM_LIMIT_BYTES),
        name="in_proj")(x, mod, w_cat)


def _dwconv_silu(x, w_ref, b_ref):
    L = x.shape[0]
    row = lax.broadcasted_iota(jnp.int32, x.shape, 0)
    acc = x * w_ref[CONV_W // 2:CONV_W // 2 + 1, :] + b_ref[...]
    for w in range(CONV_W):
        s = w - CONV_W // 2
        if s == 0:
            continue
        shifted = pltpu.roll(x, (-s) % L, 0)
        ok = (row + s >= 0) & (row + s < L)
        acc = acc + jnp.where(ok, shifted, 0.0) * w_ref[w:w + 1, :]
    return _silu(acc)


def _cumsum_rows(a, reverse):
    q = a.shape[0]
    row = lax.broadcasted_iota(jnp.int32, a.shape, 0)
    s = 1
    while s < q:
        if reverse:
            a = a + jnp.where(row < q - s, pltpu.roll(a, q - s, 0), 0.0)
        else:
            a = a + jnp.where(row >= s, pltpu.roll(a, s, 0), 0.0)
        s *= 2
    return a


def _ssd_kernel(z_ref, xs_ref, bc_ref, dt_ref, cwx_ref, cbx_ref, cwbc_ref, cbbc_ref, a_ref, dtb_ref, dsk_ref,
                ng_ref, h0_ref, y_ref, st_ref, xc_ref, bcc_ref, dtp_ref, *, n_chunks):
    Q = SSD_CHUNK
    P = SSD_HEAD_DIM
    N = SSD_STATE
    rep = SSD_HEADS // SSD_GROUPS
    xc_ref[...] = _dwconv_silu(xs_ref[0], cwx_ref, cbx_ref)
    bcc_ref[...] = _dwconv_silu(bc_ref[0], cwbc_ref, cbbc_ref)
    pre = dt_ref[0] + dtb_ref[...]
    dtp_ref[...] = jnp.maximum(pre, 0.0) + jnp.log1p(jnp.exp(-jnp.abs(pre)))
    y_ref[0] = xc_ref[...] * dsk_ref[...]
    st_ref[0] = h0_ref[0]
    ii = lax.broadcasted_iota(jnp.int32, (Q, Q), 0)
    jj = lax.broadcasted_iota(jnp.int32, (Q, Q), 1)
    nt = (((1,), (1,)), ((), ()))

    def chunk_pair(c, carry):
        for d in range(2):
            cc = c if d == 0 else n_chunks - 1 - c
            r0 = pl.multiple_of(cc * Q, Q)
            x = xc_ref[pl.ds(r0, Q), :]
            bcv = bcc_ref[pl.ds(r0, Q), :]
            dt = dtp_ref[pl.ds(r0, Q), :]
            a = dt * a_ref[...]
            cs = _cumsum_rows(a, reverse=(d == 1))
            tot = cs[Q - 1:Q, :] if d == 0 else cs[0:1, :]
            dec = jnp.exp(tot - cs)
            inn = jnp.exp(cs)
            cs_t = cs.T
            w_t = (dt * dec).T
            etot = jnp.exp(tot)
            x_t = x.T
            keep = (jj <= ii) if d == 0 else (jj >= ii)
            ys = []
            for g in range(SSD_GROUPS):
                bg = bcv[:, g * N:(g + 1) * N].astype(jnp.bfloat16)
                cg = bcv[:, SSD_BC + g * N:SSD_BC + (g + 1) * N].astype(jnp.bfloat16)
                cb = lax.dot_general(cg, bg, nt, preferred_element_type=jnp.float32)
                for hg in range(rep):
                    h = g * rep + hg
                    ln = d * SSD_HEADS + h
                    seg = cs[:, ln:ln + 1] - cs_t[ln:ln + 1, :]
                    lmat = jnp.where(keep, jnp.exp(jnp.where(keep, seg, 0.0)), 0.0)
                    xh = x[:, h * P:(h + 1) * P]
                    xd = (xh * dt[:, ln:ln + 1]).astype(jnp.bfloat16)
                    y_diag = jnp.dot((cb * lmat).astype(jnp.bfloat16), xd, preferred_element_type=jnp.float32)
                    hprev = st_ref[0, d, h]
                    y_off = lax.dot_general(cg, hprev.astype(jnp.bfloat16), nt,
                                            preferred_element_type=jnp.float32) * inn[:, ln:ln + 1]
                    ys.append(y_diag + y_off)
                    xw = (x_t[h * P:(h + 1) * P, :] * w_t[ln:ln + 1, :]).astype(jnp.bfloat16)
                    s_new = jnp.dot(xw, bg, preferred_element_type=jnp.float32)
                    st_ref[0, d, h] = hprev * etot[:, ln:ln + 1] + s_new
            y_ref[0, pl.ds(r0, Q), :] += jnp.concatenate(ys, axis=1)
        return carry

    lax.fori_loop(0, n_chunks, chunk_pair, 0)
    y = y_ref[0] * _silu(z_ref[0])
    y = y * lax.rsqrt(jnp.mean(y * y, axis=-1, keepdims=True) + EPS)
    y_ref[0] = y * ng_ref[...]


def ssd_mixer(z, xs, bc, dt_raw, conv_w, conv_b, a_log, dt_bias, d_skip, norm_g, h0):
    B, L, _ = z.shape
    n_chunks = L // SSD_CHUNK
    pad = lambda v: jnp.pad(v.reshape(1, -1), ((0, 0), (0, LANES - v.size)))
    a_row = pad(-jnp.exp(a_log))
    dtb_row = pad(dt_bias)
    dsk = jnp.repeat(d_skip, SSD_HEAD_DIM)[None, :]
    seq = lambda w: pl.BlockSpec((1, L, w), lambda b: (b, 0, 0))
    full = lambda a: pl.BlockSpec(a.shape, lambda b: (0,) * a.ndim)
    cwx, cwbc = conv_w[:, :SSD_INNER], conv_w[:, SSD_INNER:]
    cbx, cbbc = conv_b[None, :SSD_INNER], conv_b[None, SSD_INNER:]
    st_spec = pl.BlockSpec((1, 2, SSD_HEADS, SSD_HEAD_DIM, SSD_STATE), lambda b: (b, 0, 0, 0, 0))
    consts = [cwx, cbx, cwbc, cbbc, a_row, dtb_row, dsk, norm_g[None, :]]
    return pl.pallas_call(
        functools.partial(_ssd_kernel, n_chunks=n_chunks),
        grid=(B,),
        in_specs=[seq(SSD_INNER), seq(SSD_INNER), seq(2 * SSD_BC), seq(DT_PAD)] + [full(a) for a in consts]
        + [st_spec],
        out_specs=[seq(SSD_INNER), st_spec],
        out_shape=[jax.ShapeDtypeStruct((B, L, SSD_INNER), jnp.float32),
                   jax.ShapeDtypeStruct(h0.shape, jnp.float32)],
        scratch_shapes=[pltpu.VMEM((L, SSD_INNER), jnp.float32), pltpu.VMEM((L, 2 * SSD_BC), jnp.float32),
                        pltpu.VMEM((L, DT_PAD), jnp.float32)],
        compiler_params=pltpu.CompilerParams(dimension_semantics=("arbitrary",), vmem_limit_bytes=VMEM_LIMIT_BYTES),
        name="ssd_mixer")(z, xs, bc, dt_raw, *consts, h0)


def _head_rms(x, gain_row, n_heads):
    parts = []
    for h in range(n_heads):
        xh = x[:, h * HEAD_DIM:(h + 1) * HEAD_DIM]
        parts.append(xh * lax.rsqrt(jnp.mean(xh * xh, axis=-1, keepdims=True) + EPS))
    return jnp.concatenate(parts, axis=1) * gain_row


def _rope(x, cos, sin_signed, first16):
    n = x.shape[1]
    partner = jnp.where(first16, pltpu.roll(x, n - 16, 1), pltpu.roll(x, 16, 1))
    return x * cos + partner * sin_signed


def _softmax_rows(s):
    m = jnp.max(s, axis=-1, keepdims=True)
    e = jnp.exp(s - m)
    return e / jnp.sum(e, axis=-1, keepdims=True)


def _dense_attn_kernel(*refs, n_heads, n_kv, has_cache, norm, rope, q_blk):
    it = iter(refs)
    q_ref, k_ref, v_ref = next(it), next(it), next(it)
    ck_ref = cv_ref = qg_ref = kg_ref = cq_ref = sq_ref = ckk_ref = skk_ref = None
    if has_cache:
        ck_ref, cv_ref = next(it), next(it)
    if norm:
        qg_ref, kg_ref = next(it), next(it)
    if rope:
        cq_ref, sq_ref, ckk_ref, skk_ref = next(it), next(it), next(it), next(it)
    o_ref = next(it)
    kn_ref = next(it) if norm else None

    q = q_ref[0]
    k = k_ref[0]
    if norm:
        q = _head_rms(q, qg_ref[...], n_heads)
        k = _head_rms(k, kg_ref[...], n_kv)
        kn_ref[0] = k
    if rope:
        lane_q = lax.broadcasted_iota(jnp.int32, q.shape, 1)
        lane_k = lax.broadcasted_iota(jnp.int32, k.shape, 1)
        q = _rope(q, cq_ref[...], sq_ref[...], lax.bitwise_and(lane_q, 31) < 16)
        k = _rope(k, ckk_ref[...], skk_ref[...], lax.bitwise_and(lane_k, 31) < 16)
    qb = q.astype(jnp.bfloat16)
    kb = k.astype(jnp.bfloat16)
    vb = v_ref[0].astype(jnp.bfloat16)
    if has_cache:
        kb = jnp.concatenate([ck_ref[0].astype(jnp.bfloat16), kb], axis=0)
        vb = jnp.concatenate([cv_ref[0].astype(jnp.bfloat16), vb], axis=0)
    lq = q.shape[0]
    grp = n_heads // n_kv
    scale = HEAD_DIM ** -0.5
    outs = []
    for h in range(n_heads):
        kv = h // grp
        kh = kb[:, kv * HEAD_DIM:(kv + 1) * HEAD_DIM]
        vh = vb[:, kv * HEAD_DIM:(kv + 1) * HEAD_DIM]
        rows = []
        for r0 in range(0, lq, q_blk):
            qh = qb[r0:r0 + q_blk, h * HEAD_DIM:(h + 1) * HEAD_DIM]
            s = lax.dot_general(qh, kh, (((1,), (1,)), ((), ())), preferred_element_type=jnp.float32) * scale
            p = _softmax_rows(s).astype(jnp.bfloat16)
            rows.append(jnp.dot(p, vh, preferred_element_type=jnp.float32))
        outs.append(rows[0] if len(rows) == 1 else jnp.concatenate(rows, axis=0))
    o_ref[0] = jnp.concatenate(outs, axis=1)


def dense_attention(q, k, v, cache_k=None, cache_v=None, q_gain=None, k_gain=None, rope_tabs=None):
    B, L, qw = q.shape
    kw = k.shape[2]
    n_heads, n_kv = qw // HEAD_DIM, kw // HEAD_DIM
    has_cache, norm, rope = cache_k is not None, q_gain is not None, rope_tabs is not None
    args = [q, k, v]
    specs = [pl.BlockSpec((1, L, qw), lambda b: (b, 0, 0)),
             pl.BlockSpec((1, L, kw), lambda b: (b, 0, 0)),
             pl.BlockSpec((1, L, kw), lambda b: (b, 0, 0))]
    if has_cache:
        lc = cache_k.shape[1]
        args += [cache_k, cache_v]
        specs += [pl.BlockSpec((1, lc, kw), lambda b: (b, 0, 0))] * 2
    if norm:
        args += [jnp.tile(q_gain, n_heads)[None, :], jnp.tile(k_gain, n_kv)[None, :]]
        specs += [pl.BlockSpec((1, qw), lambda b: (0, 0)), pl.BlockSpec((1, kw), lambda b: (0, 0))]
    if rope:
        args += list(rope_tabs)
        specs += [pl.BlockSpec((L, qw), lambda b: (0, 0))] * 2 + [pl.BlockSpec((L, kw), lambda b: (0, 0))] * 2
    out_shape = [jax.ShapeDtypeStruct((B, L, qw), jnp.float32)]
    out_specs = [pl.BlockSpec((1, L, qw), lambda b: (b, 0, 0))]
    if norm:
        out_shape.append(jax.ShapeDtypeStruct((B, L, kw), jnp.float32))
        out_specs.append(pl.BlockSpec((1, L, kw), lambda b: (b, 0, 0)))
    kern = functools.partial(_dense_attn_kernel, n_heads=n_heads, n_kv=n_kv, has_cache=has_cache, norm=norm,
                             rope=rope, q_blk=min(L, 256))
    res = pl.pallas_call(kern, grid=(B,), in_specs=specs, out_specs=out_specs, out_shape=out_shape,
                         compiler_params=pltpu.CompilerParams(dimension_semantics=("arbitrary",),
                                                              vmem_limit_bytes=VMEM_LIMIT_BYTES),
                         name="dense_attention")(*args)
    return res if norm else res[0]


def rope_tables(L, n_heads):
    t = np.arange(L)
    pos = np.stack([t // GRID_W, t % GRID_W], axis=1).astype(np.float32)
    half = HEAD_DIM // 2
    inv = 1.0 / (ROPE_THETA ** (jnp.arange(0, half, 2, dtype=jnp.float32) / half))
    ang = jnp.asarray(pos)[:, :, None] * inv[None, None, :]
    cos, sin = jnp.cos(ang), jnp.sin(ang)
    cos64 = jnp.concatenate([cos, cos], axis=2).reshape(L, HEAD_DIM)
    sin64 = jnp.concatenate([-sin, sin], axis=2).reshape(L, HEAD_DIM)
    return jnp.tile(cos64, (1, n_heads)), jnp.tile(sin64, (1, n_heads))


def na_bias_table(rpb):
    qcol = np.arange(GRID_W)[:, None]
    kc = np.arange(GRID_W)[None, :]
    qstart = np.clip(qcol - NA_COLS // 2, 0, GRID_W - NA_COLS)
    valid = (kc >= qstart) & (kc < qstart + NA_COLS)
    dc = np.clip(kc - qcol + NA_COLS - 1, 0, 2 * NA_COLS - 2)
    dr = np.arange(NA_ROWS)[:, None] + np.arange(NA_ROWS)[None, :]
    t = rpb[:, dr][:, :, :, dc]
    t = jnp.where(jnp.asarray(valid)[None, None, None], t, NEG_BIG)
    return t.transpose(0, 1, 3, 2, 4).reshape(rpb.shape[0], NA_ROWS, GRID_W, NA_ROWS * GRID_W)


def _na_kernel(q_ref, k_ref, v_ref, ck_ref, cv_ref, bias_ref, o_ref, *, rows):
    scale = HEAD_DIM ** -0.5
    qb = q_ref[0].astype(jnp.bfloat16)
    kb = k_ref[0].astype(jnp.bfloat16)
    vb = v_ref[0].astype(jnp.bfloat16)
    ckb = ck_ref[0].astype(jnp.bfloat16)
    cvb = cv_ref[0].astype(jnp.bfloat16)
    n_loc = NA_ROWS * GRID_W
    nt = (((1,), (1,)), ((), ()))
    outs = []
    for hh in range(2):
        sl = slice(hh * HEAD_DIM, (hh + 1) * HEAD_DIM)
        rows_out = []
        for r in range(rows):
            kr0 = min(max(r - NA_ROWS // 2, 0), rows - NA_ROWS)
            off = kr0 - r + NA_ROWS - 1
            qr = qb[r * GRID_W:(r + 1) * GRID_W, sl]
            kl = kb[kr0 * GRID_W:kr0 * GRID_W + n_loc, sl]
            vl = vb[kr0 * GRID_W:kr0 * GRID_W + n_loc, sl]
            s_loc = lax.dot_general(qr, kl, nt, preferred_element_type=jnp.float32) * scale + bias_ref[hh, off]
            s_ctx = lax.dot_general(qr, ckb[:, sl], nt, preferred_element_type=jnp.float32) * scale
            m = jnp.maximum(jnp.max(s_loc, axis=-1, keepdims=True), jnp.max(s_ctx, axis=-1, keepdims=True))
            e_loc = jnp.exp(s_loc - m)
            e_ctx = jnp.exp(s_ctx - m)
            den = jnp.sum(e_loc, axis=-1, keepdims=True) + jnp.sum(e_ctx, axis=-1, keepdims=True)
            p_loc = (e_loc / den).astype(jnp.bfloat16)
            p_ctx = (e_ctx / den).astype(jnp.bfloat16)
            rows_out.append(jnp.dot(p_loc, vl, preferred_element_type=jnp.float32)
                            + jnp.dot(p_ctx, cvb[:, sl], preferred_element_type=jnp.float32))
        outs.append(jnp.concatenate(rows_out, axis=0))
    o_ref[0] = jnp.concatenate(outs, axis=1)


def neighbourhood_attention(q, k, v, ctx_k, ctx_v, bias_tbl):
    B, L, w = q.shape
    lc = ctx_k.shape[1]
    n_pairs = w // LANES
    rows = L // GRID_W
    assert rows >= NA_ROWS
    kern = functools.partial(_na_kernel, rows=rows)
    blk = lambda n: pl.BlockSpec((1, n, LANES), lambda hp, b: (b, 0, hp))
    return pl.pallas_call(
        kern, grid=(n_pairs, B),
        in_specs=[blk(L), blk(L), blk(L), blk(lc), blk(lc),
                  pl.BlockSpec((2, NA_ROWS, GRID_W, NA_ROWS * GRID_W), lambda hp, b: (hp, 0, 0, 0))],
        out_specs=blk(L),
        out_shape=jax.ShapeDtypeStruct((B, L, w), jnp.float32),
        compiler_params=pltpu.CompilerParams(dimension_semantics=("arbitrary", "arbitrary"),
                                             vmem_limit_bytes=VMEM_LIMIT_BYTES),
        name="neighbourhood_attention")(q, k, v, ctx_k, ctx_v, bias_tbl)


def _merge_kernel(ya_ref, yb_ref, yc_ref, gate_ref, x_ref, mod_ref, wb_ref, wo_ref, g_ref, b_ref, wqt_ref,
                  x1_ref, h2_ref, qt_ref):
    m = mod_ref[0]
    merged = None
    for i, y_ref in enumerate((ya_ref, yb_ref, yc_ref)):
        br = jnp.dot(y_ref[...].astype(jnp.bfloat16), wb_ref[i], preferred_element_type=jnp.float32)
        t = _sigmoid(gate_ref[:, i * D_MODEL:(i + 1) * D_MODEL]) * br
        merged = t if merged is None else merged + t
    y = jnp.dot(merged.astype(jnp.bfloat16), wo_ref[...], preferred_element_type=jnp.float32)
    x1 = _ln(DN_ALPHA * x_ref[...] + m[:, 2 * D_MODEL:3 * D_MODEL] * y) * g_ref[...] + b_ref[...]
    x1_ref[...] = x1
    h2 = (_ln(x1) * (1.0 + m[:, 4 * D_MODEL:5 * D_MODEL]) + m[:, 3 * D_MODEL:4 * D_MODEL]).astype(jnp.bfloat16)
    h2_ref[...] = h2
    qt_ref[...] = lax.dot_general(wqt_ref[...], h2, (((1,), (1,)), ((), ())), preferred_element_type=jnp.float32)


def merge_branches(ya, yb, yc, gate, x, mod, w_branch, w_out, ln_g, ln_b, wq_t, is_ctx):
    T, d = x.shape
    nq = wq_t.shape[0]
    row = lambda w: pl.BlockSpec((ROW_TILE, w), lambda i: (i, 0))
    full = lambda a: pl.BlockSpec(a.shape, lambda i: (0,) * a.ndim)
    return pl.pallas_call(
        _merge_kernel, grid=(T // ROW_TILE,),
        in_specs=[row(512), row(512), row(512), row(N_BRANCH * d), row(d),
                  pl.BlockSpec((1, 1, mod.shape[2]), lambda i: (_cond_row(i, is_ctx, ROW_TILE), 0, 0)),
                  full(w_branch), full(w_out), full(ln_g), full(ln_b), full(wq_t)],
        out_specs=[row(d), row(d), pl.BlockSpec((nq, ROW_TILE), lambda i: (0, i))],
        out_shape=[jax.ShapeDtypeStruct((T, d), jnp.float32), jax.ShapeDtypeStruct((T, d), jnp.bfloat16),
                   jax.ShapeDtypeStruct((nq, T), jnp.float32)],
        compiler_params=pltpu.CompilerParams(dimension_semantics=("arbitrary",), vmem_limit_bytes=VMEM_LIMIT_BYTES),
        name="merge_branches")(ya, yb, yc, gate, x, mod, w_branch, w_out, ln_g, ln_b, wq_t)


def _topk_rows(s, k):
    n = s.shape[0]
    iota = lax.broadcasted_iota(jnp.int32, s.shape, 0)
    vals, idxs = [], []
    for _ in range(k):
        m = jnp.max(s, axis=0, keepdims=True)
        idx = jnp.min(jnp.where(s == m, iota, n), axis=0, keepdims=True)
        vals.append(m)
        idxs.append(idx)
        s = jnp.where(iota == idx, -jnp.inf, s)
    return jnp.concatenate(vals, axis=0), jnp.concatenate(idxs, axis=0)


def _select_rows(table, sel):
    out = jnp.zeros(sel.shape, table.dtype)
    for a in range(table.shape[0]):
        out = jnp.where(sel == a, table[a:a + 1, :], out)
    return out


def _peer_topk_kernel(qt_ref, keys_ref, et_ref, gt_ref):
    def head(h, carry):
        sv, si = [], []
        for p in range(2):
            q = qt_ref[pl.ds(pl.multiple_of(h * 256 + p * 128, 128), 128), :].astype(jnp.bfloat16)
            s = jnp.dot(keys_ref[h, p], q, preferred_element_type=jnp.float32)
            v, i = _topk_rows(s, PEER_TOPK)
            sv.append(v)
            si.append(i)
        cand = jnp.concatenate([sv[0][a:a + 1, :] + sv[1] for a in range(PEER_TOPK)], axis=0)
        cv, ci = _topk_rows(cand, PEER_TOPK)
        i_sel = _select_rows(si[0], lax.shift_right_logical(ci, 4))
        j_sel = _select_rows(si[1], lax.bitwise_and(ci, PEER_TOPK - 1))
        ex = jnp.exp(cv - cv[0:1, :])
        g = ex / jnp.sum(ex, axis=0, keepdims=True)
        row = pl.multiple_of(h * PEER_TOPK, PEER_TOPK)
        et_ref[pl.ds(row, PEER_TOPK), :] = i_sel * PEER_KEYS + j_sel
        gt_ref[pl.ds(row, PEER_TOPK), :] = g
        return carry

    lax.fori_loop(0, PEER_HEADS, head, 0)


def peer_topk(qt, keys_bf16):
    T = qt.shape[1]
    return pl.pallas_call(
        _peer_topk_kernel,
        grid=(T // TOPK_TOK,),
        in_specs=[pl.BlockSpec((qt.shape[0], TOPK_TOK), lambda i: (0, i)),
                  pl.BlockSpec(keys_bf16.shape, lambda i: (0, 0, 0, 0))],
        out_specs=[pl.BlockSpec((PEER_HEADS * PEER_TOPK, TOPK_TOK), lambda i: (0, i)),
                   pl.BlockSpec((PEER_HEADS * PEER_TOPK, TOPK_TOK), lambda i: (0, i))],
        out_shape=[jax.ShapeDtypeStruct((PEER_HEADS * PEER_TOPK, T), jnp.int32),
                   jax.ShapeDtypeStruct((PEER_HEADS * PEER_TOPK, T), jnp.float32)],
        compiler_params=pltpu.CompilerParams(dimension_semantics=("arbitrary",)),
        name="peer_topk",
    )(qt, keys_bf16)


def _peer_dense_kernel(x_ref, e_ref, g_ref, ut_ref, v_ref, x1_ref, mod_ref, lg_ref, lb_ref, o_ref, g3_ref, acc_ref,
                       *, t_blk, te):
    j = pl.program_id(1)
    nk = PEER_KEYS
    n_i = te // nk

    @pl.when(j == 0)
    def _build():
        acc_ref[...] = jnp.zeros_like(acc_ref)
        sub_iota = lax.broadcasted_iota(jnp.int32, (nk, nk), 0)

        def body(t, carry):
            e_row = e_ref[pl.ds(t, 1), :]
            g_row = g_ref[pl.ds(t, 1), :]
            i_row = lax.shift_right_logical(e_row, 7)
            j_row = lax.bitwise_and(e_row, nk - 1)
            g_hi = g_row.astype(jnp.bfloat16).astype(jnp.float32)
            g_lo = g_row - g_hi
            sel_i = sub_iota == i_row
            a_hi = jnp.where(sel_i, g_hi, 0.0).astype(jnp.bfloat16)
            a_lo = jnp.where(sel_i, g_lo, 0.0).astype(jnp.bfloat16)
            b_t = jnp.where(sub_iota == j_row, 1.0, 0.0).astype(jnp.bfloat16)
            lhs = jnp.concatenate([a_hi, a_lo], axis=1)
            rhs = jnp.concatenate([b_t, b_t], axis=1)
            gt = lax.dot_general(lhs, rhs, (((1,), (1,)), ((), ())), preferred_element_type=jnp.float32)
            g3_ref[pl.ds(pl.multiple_of(t * G3_PITCH, 8), nk), :] = gt
            return carry

        lax.fori_loop(0, t_blk, body, 0, unroll=8)

    s = jnp.dot(x_ref[...], ut_ref[...], preferred_element_type=jnp.float32)
    parts = []
    for k in range(n_i):
        gi = g3_ref[pl.ds(j * n_i + k, t_blk, stride=G3_PITCH), :]
        parts.append((_gelu_tanh(s[:, k * nk:(k + 1) * nk]) * gi).astype(jnp.bfloat16))
    a = jnp.concatenate(parts, axis=1)
    acc_ref[...] += jnp.dot(a, v_ref[...], preferred_element_type=jnp.float32)

    @pl.when(j == pl.num_programs(1) - 1)
    def _fin():
        gate2 = mod_ref[0][:, 5 * D_MODEL:6 * D_MODEL]
        o_ref[...] = _ln(DN_ALPHA * x1_ref[...] + gate2 * acc_ref[...]) * lg_ref[...] + lb_ref[...]


def peer_dense(x_bf16, e, g, ut_bf16, v_bf16, x1, mod, ln_g, ln_b, is_ctx):
    T, D = x_bf16.shape
    NE = v_bf16.shape[0]
    t_blk, te = PEER_TOK_BLOCK, PEER_EXPERT_TILE
    kern = functools.partial(_peer_dense_kernel, t_blk=t_blk, te=te)
    tok = lambda w: pl.BlockSpec((t_blk, w), lambda i, j: (i, 0))
    return pl.pallas_call(
        kern,
        grid=(T // t_blk, NE // te),
        in_specs=[tok(D), tok(LANES), tok(LANES),
                  pl.BlockSpec((D, te), lambda i, j: (0, j)),
                  pl.BlockSpec((te, D), lambda i, j: (j, 0)),
                  tok(D),
                  pl.BlockSpec((1, 1, mod.shape[2]), lambda i, j: (_cond_row(i, is_ctx, t_blk), 0, 0)),
                  pl.BlockSpec((1, D), lambda i, j: (0, 0)),
                  pl.BlockSpec((1, D), lambda i, j: (0, 0))],
        out_specs=tok(D),
        out_shape=jax.ShapeDtypeStruct((T, D), jnp.float32),
        scratch_shapes=[pltpu.VMEM((t_blk * G3_PITCH, PEER_KEYS), jnp.float32),
                        pltpu.VMEM((t_blk, D), jnp.float32)],
        compiler_params=pltpu.CompilerParams(dimension_semantics=("arbitrary", "arbitrary"),
                                             vmem_limit_bytes=VMEM_LIMIT_BYTES),
        name="peer_dense",
    )(x_bf16, e, g, ut_bf16, v_bf16, x1, mod, ln_g, ln_b)


def _in_proj_weight(w_in_l):
    offs = np.cumsum((0,) + IN_SIZES)
    col = lambda i: w_in_l[:, offs[i]:offs[i + 1]]
    dt_pad = jnp.pad(col(4), ((0, 0), (0, DT_PAD - IN_SIZES[4])))
    parts = [col(0), col(1), col(2), col(3), dt_pad] + [col(i) for i in range(5, 12)]
    return jnp.concatenate(parts, axis=1).astype(jnp.bfloat16)


def kernel(x_prompt, x_sample, cache_na_k, cache_na_v, cache_gqa_k, cache_gqa_v, state_ssd, c, c_ctx,
           w_mod, b_mod, w_in, conv_w, conv_b, ssd_a_log, ssd_dt_bias, ssd_d, ssd_norm_g, na_rpb,
           gqa_q_norm, gqa_k_norm, w_branch, w_out, ln1_g, ln1_b, ln2_g, ln2_b,
           peer_wq, peer_keys, peer_u, peer_v):
    bf = jnp.bfloat16
    cond = jnp.zeros((COND_ROWS, D_MODEL), jnp.float32).at[0].set(c_ctx).at[1:1 + DEC_BATCH].set(c)
    mod_all = adaln_table(cond, w_mod, b_mod)
    rope_q = rope_tables(DEC_SEQ, GQA_HEADS)
    rope_k = rope_tables(DEC_SEQ, GQA_KV_HEADS)
    zero_state = jnp.zeros((BATCH, 2, SSD_HEADS, SSD_HEAD_DIM, SSD_STATE), jnp.float32)

    streams = {True: x_prompt.reshape(BATCH * SEQ, D_MODEL), False: x_sample.reshape(DEC_BATCH * DEC_SEQ, D_MODEL)}
    new_na_k, new_na_v, new_g_k, new_g_v, new_ssd = [], [], [], [], []
    for l in range(DEPTH):
        mod = mod_all[l].reshape(COND_ROWS, 1, 6 * D_MODEL)
        w_cat = _in_proj_weight(w_in[l])
        wb, wo, wq_t = w_branch[l].astype(bf), w_out[l].astype(bf), peer_wq[l].T.astype(bf)
        keys_b = peer_keys[l].astype(bf)
        ut, vb = peer_u[l].T.astype(bf), peer_v[l].astype(bf)
        bias_tbl = na_bias_table(na_rpb[l])
        ssd_args = (conv_w[l], conv_b[l], ssd_a_log[l], ssd_dt_bias[l], ssd_d[l], ssd_norm_g[l])
        for is_ctx in (True, False):
            x = streams[is_ctx]
            nb, sl = (BATCH, SEQ) if is_ctx else (DEC_BATCH, DEC_SEQ)
            z, xs, bc, dt, naq, nak, nav, gq, gk, gv, gate = in_proj(x, mod, w_cat, is_ctx)
            r3 = lambda a: a.reshape(nb, sl, a.shape[-1])
            if is_ctx:
                ya, st = ssd_mixer(r3(z), r3(xs), r3(bc), r3(dt), *ssd_args, zero_state)
                yb = dense_attention(r3(naq), r3(nak), r3(nav))
                yc, gk_n = dense_attention(r3(gq), r3(gk), r3(gv), q_gain=gqa_q_norm[l], k_gain=gqa_k_norm[l])
                new_na_k.append(nak.reshape(BATCH, SEQ, NA_HEADS, HEAD_DIM))
                new_na_v.append(nav.reshape(BATCH, SEQ, NA_HEADS, HEAD_DIM))
                new_g_k.append(gk_n.reshape(BATCH, SEQ, GQA_KV_HEADS, HEAD_DIM))
                new_g_v.append(gv.reshape(BATCH, SEQ, GQA_KV_HEADS, HEAD_DIM))
                new_ssd.append(st)
            else:
                ya, _ = ssd_mixer(r3(z), r3(xs), r3(bc), r3(dt), *ssd_args, state_ssd[:, l])
                yb = neighbourhood_attention(r3(naq), r3(nak), r3(nav),
                                             cache_na_k[:, l].reshape(DEC_BATCH, PAST_LEN, NA_HEADS * HEAD_DIM),
                                             cache_na_v[:, l].reshape(DEC_BATCH, PAST_LEN, NA_HEADS * HEAD_DIM),
                                             bias_tbl)
                yc, _ = dense_attention(r3(gq), r3(gk), r3(gv),
                                        cache_gqa_k[:, l].reshape(DEC_BATCH, PAST_LEN, GQA_KV_HEADS * HEAD_DIM),
                                        cache_gqa_v[:, l].reshape(DEC_BATCH, PAST_LEN, GQA_KV_HEADS * HEAD_DIM),
                                        gqa_q_norm[l], gqa_k_norm[l], rope_q + rope_k)
            r2 = lambda a: a.reshape(nb * sl, a.shape[-1])
            x1, h2, qt = merge_branches(r2(ya), r2(yb), r2(yc), gate, x, mod, wb, wo,
                                        ln1_g[l][None, :], ln1_b[l][None, :], wq_t, is_ctx)
            et, gt = peer_topk(qt, keys_b)
            streams[is_ctx] = peer_dense(h2, et.T, gt.T, ut, vb, x1, mod,
                                         ln2_g[l][None, :], ln2_b[l][None, :], is_ctx)

    return (streams[True].reshape(BATCH, SEQ, D_MODEL), streams[False].reshape(DEC_BATCH, DEC_SEQ, D_MODEL),
            jnp.stack(new_na_k, axis=1), jnp.stack(new_na_v, axis=1),
            jnp.stack(new_g_k, axis=1), jnp.stack(new_g_v, axis=1), jnp.stack(new_ssd, axis=1))
```

```python
import functools

import jax
import jax.numpy as jnp
import numpy as np
from jax import lax
from jax.experimental import pallas as pl
from jax.experimental.pallas import tpu as pltpu

D_MODEL = 1024
BATCH = 16
SEQ = 256
DEPTH = 4
DEC_BATCH = 4
DEC_SEQ = 1024
PAST_LEN = 256
GRID_W = 64
HEAD_DIM = 64
N_BRANCH = 3
SSD_HEADS = 8
SSD_HEAD_DIM = 64
SSD_INNER = SSD_HEADS * SSD_HEAD_DIM
SSD_GROUPS = 2
SSD_STATE = 64
SSD_BC = SSD_GROUPS * SSD_STATE
SSD_CHUNK = 128
CONV_W = 5
NA_HEADS = 8
NA_ROWS = 8
NA_COLS = 16
GQA_HEADS = 8
GQA_KV_HEADS = 2
ROPE_THETA = 10000.0
PEER_HEADS = 8
PEER_KEYS = 128
PEER_TOPK = 16
DN_ALPHA = (2 * DEPTH) ** 0.25
EPS = 1e-6
IN_SIZES = (SSD_INNER, SSD_INNER, SSD_BC, SSD_BC, 2 * SSD_HEADS,
            NA_HEADS * HEAD_DIM, NA_HEADS * HEAD_DIM, NA_HEADS * HEAD_DIM,
            GQA_HEADS * HEAD_DIM, GQA_KV_HEADS * HEAD_DIM, GQA_KV_HEADS * HEAD_DIM,
            N_BRANCH * D_MODEL)

LANES = 128
VMEM_LIMIT_BYTES = 56 * 1024 * 1024
NEG_BIG = -1e30
ROW_TILE = 256
COND_ROWS = 8
DT_PAD = LANES
IN_OUT_WIDTHS = (SSD_INNER, SSD_INNER, 2 * SSD_BC, DT_PAD, 512, 512, 512, 512, 128, 128, N_BRANCH * D_MODEL)
PEER_TOK_BLOCK = 512
PEER_EXPERT_TILE = 512
G3_PITCH = 136
TOPK_TOK = LANES


def _cond_row(i, is_ctx, rows_per_step):
    if is_ctx:
        return 0
    return 1 + i // (DEC_SEQ // rows_per_step)


def _ln(x):
    mu = jnp.mean(x, axis=-1, keepdims=True)
    xc = x - mu
    var = jnp.mean(xc * xc, axis=-1, keepdims=True)
    return xc * lax.rsqrt(var + EPS)


def _silu(x):
    return x / (1.0 + jnp.exp(-x))


def _sigmoid(x):
    return 1.0 / (1.0 + jnp.exp(-x))


def _gelu_tanh(x):
    return 0.5 * x * (1.0 + jnp.tanh(0.7978845608028654 * (x + 0.044715 * (x * x * x))))


def _adaln_kernel(c_ref, w_ref, b_ref, o_ref):
    a = _silu(c_ref[...]).astype(jnp.bfloat16)
    o_ref[0] = jnp.dot(a, w_ref[0].astype(jnp.bfloat16), preferred_element_type=jnp.float32) + b_ref[0]


def adaln_table(cond, w_mod, b_mod, tn=1536):
    depth, d, n = w_mod.shape
    return pl.pallas_call(
        _adaln_kernel, grid=(depth, n // tn),
        in_specs=[pl.BlockSpec((COND_ROWS, d), lambda l, j: (0, 0)),
                  pl.BlockSpec((1, d, tn), lambda l, j: (l, 0, j)),
                  pl.BlockSpec((1, 1, tn), lambda l, j: (l, 0, j))],
        out_specs=pl.BlockSpec((1, COND_ROWS, tn), lambda l, j: (l, 0, j)),
        out_shape=jax.ShapeDtypeStruct((depth, COND_ROWS, n), jnp.float32),
        compiler_params=pltpu.CompilerParams(dimension_semantics=("arbitrary", "arbitrary"),
                                             vmem_limit_bytes=VMEM_LIMIT_BYTES),
        name="adaln_table")(cond, w_mod, b_mod.reshape(depth, 1, n))


def _in_proj_kernel(x_ref, mod_ref, w_ref, *o_refs):
    m = mod_ref[0]
    h = _ln(x_ref[...]) * (1.0 + m[:, D_MODEL:2 * D_MODEL]) + m[:, 0:D_MODEL]
    p = jnp.dot(h.astype(jnp.bfloat16), w_ref[...], preferred_element_type=jnp.float32)
    off = 0
    for o_ref, w in zip(o_refs, IN_OUT_WIDTHS):
        o_ref[...] = p[:, off:off + w]
        off += w


def in_proj(x, mod, w_cat, is_ctx):
    T, d = x.shape
    n = w_cat.shape[1]
    return pl.pallas_call(
        _in_proj_kernel, grid=(T // ROW_TILE,),
        in_specs=[pl.BlockSpec((ROW_TILE, d), lambda i: (i, 0)),
                  pl.BlockSpec((1, 1, mod.shape[2]), lambda i: (_cond_row(i, is_ctx, ROW_TILE), 0, 0)),
                  pl.BlockSpec((d, n), lambda i: (0, 0), pipeline_mode=pl.Buffered(1))],
        out_specs=[pl.BlockSpec((ROW_TILE, w), lambda i: (i, 0)) for w in IN_OUT_WIDTHS],
        out_shape=[jax.ShapeDtypeStruct((T, w), jnp.float32) for w in IN_OUT_WIDTHS],
        compiler_params=pltpu.CompilerParams(dimension_semantics=("arbitrary",), vmem_limit_bytes=VMEM_LIMIT_BYTES),
        name="in_proj")(x, mod, w_cat)


def _dwconv_silu(x, w_ref, b_ref):
    L = x.shape[0]
    row = lax.broadcasted_iota(jnp.int32, x.shape, 0)
    acc = x * w_ref[CONV_W // 2:CONV_W // 2 + 1, :] + b_ref[...]
    for w in range(CONV_W):
        s = w - CONV_W // 2
        if s == 0:
            continue
        shifted = pltpu.roll(x, (-s) % L, 0)
        ok = (row + s >= 0) & (row + s < L)
        acc = acc + jnp.where(ok, shifted, 0.0) * w_ref[w:w + 1, :]
    return _silu(acc)


def _cumsum_rows(a, reverse):
    q = a.shape[0]
    row = lax.broadcasted_iota(jnp.int32, a.shape, 0)
    s = 1
    while s < q:
        if reverse:
            a = a + jnp.where(row < q - s, pltpu.roll(a, q - s, 0), 0.0)
        else:
            a = a + jnp.where(row >= s, pltpu.roll(a, s, 0), 0.0)
        s *= 2
    return a


def _ssd_kernel(z_ref, xs_ref, bc_ref, dt_ref, cwx_ref, cbx_ref, cwbc_ref, cbbc_ref, a_ref, dtb_ref, dsk_ref,
                ng_ref, h0_ref, y_ref, st_ref, xc_ref, bcc_ref, dtp_ref, *, n_chunks):
    Q = SSD_CHUNK
    P = SSD_HEAD_DIM
    N = SSD_STATE
    rep = SSD_HEADS // SSD_GROUPS
    xc_ref[...] = _dwconv_silu(xs_ref[0], cwx_ref, cbx_ref)
    bcc_ref[...] = _dwconv_silu(bc_ref[0], cwbc_ref, cbbc_ref)
    pre = dt_ref[0] + dtb_ref[...]
    dtp_ref[...] = jnp.maximum(pre, 0.0) + jnp.log1p(jnp.exp(-jnp.abs(pre)))
    y_ref[0] = xc_ref[...] * dsk_ref[...]
    st_ref[0] = h0_ref[0]
    ii = lax.broadcasted_iota(jnp.int32, (Q, Q), 0)
    jj = lax.broadcasted_iota(jnp.int32, (Q, Q), 1)
    nt = (((1,), (1,)), ((), ()))

    def chunk_pair(c, carry):
        for d in range(2):
            cc = c if d == 0 else n_chunks - 1 - c
            r0 = pl.multiple_of(cc * Q, Q)
            x = xc_ref[pl.ds(r0, Q), :]
            bcv = bcc_ref[pl.ds(r0, Q), :]
            dt = dtp_ref[pl.ds(r0, Q), :]
            a = dt * a_ref[...]
            cs = _cumsum_rows(a, reverse=(d == 1))
            tot = cs[Q - 1:Q, :] if d == 0 else cs[0:1, :]
            dec = jnp.exp(tot - cs)
            inn = jnp.exp(cs)
            cs_t = cs.T
            w_t = (dt * dec).T
            etot = jnp.exp(tot)
            x_t = x.T
            keep = (jj <= ii) if d == 0 else (jj >= ii)
            ys = []
            for g in range(SSD_GROUPS):
                bg = bcv[:, g * N:(g + 1) * N].astype(jnp.bfloat16)
                cg = bcv[:, SSD_BC + g * N:SSD_BC + (g + 1) * N].astype(jnp.bfloat16)
                cb = lax.dot_general(cg, bg, nt, preferred_element_type=jnp.float32)
                for hg in range(rep):
                    h = g * rep + hg
                    ln = d * SSD_HEADS + h
                    seg = cs[:, ln:ln + 1] - cs_t[ln:ln + 1, :]
                    lmat = jnp.where(keep, jnp.exp(jnp.where(keep, seg, 0.0)), 0.0)
                    xh = x[:, h * P:(h + 1) * P]
                    xd = (xh * dt[:, ln:ln + 1]).astype(jnp.bfloat16)
                    y_diag = jnp.dot((cb * lmat).astype(jnp.bfloat16), xd, preferred_element_type=jnp.float32)
                    hprev = st_ref[0, d, h]
                    y_off = lax.dot_general(cg, hprev.astype(jnp.bfloat16), nt,
                                            preferred_element_type=jnp.float32) * inn[:, ln:ln + 1]
                    ys.append(y_diag + y_off)
                    xw = (x_t[h * P:(h + 1) * P, :] * w_t[ln:ln + 1, :]).astype(jnp.bfloat16)
                    s_new = jnp.dot(xw, bg, preferred_element_type=jnp.float32)
                    st_ref[0, d, h] = hprev * etot[:, ln:ln + 1] + s_new
            y_ref[0, pl.ds(r0, Q), :] += jnp.concatenate(ys, axis=1)
        return carry

    lax.fori_loop(0, n_chunks, chunk_pair, 0)
    y = y_ref[0] * _silu(z_ref[0])
    y = y * lax.rsqrt(jnp.mean(y * y, axis=-1, keepdims=True) + EPS)
    y_ref[0] = y * ng_ref[...]


def ssd_mixer(z, xs, bc, dt_raw, conv_w, conv_b, a_log, dt_bias, d_skip, norm_g, h0):
    B, L, _ = z.shape
    n_chunks = L // SSD_CHUNK
    pad = lambda v: jnp.pad(v.reshape(1, -1), ((0, 0), (0, LANES - v.size)))
    a_row = pad(-jnp.exp(a_log))
    dtb_row = pad(dt_bias)
    dsk = jnp.repeat(d_skip, SSD_HEAD_DIM)[None, :]
    seq = lambda w: pl.BlockSpec((1, L, w), lambda b: (b, 0, 0))
    full = lambda a: pl.BlockSpec(a.shape, lambda b: (0,) * a.ndim)
    cwx, cwbc = conv_w[:, :SSD_INNER], conv_w[:, SSD_INNER:]
    cbx, cbbc = conv_b[None, :SSD_INNER], conv_b[None, SSD_INNER:]
    st_spec = pl.BlockSpec((1, 2, SSD_HEADS, SSD_HEAD_DIM, SSD_STATE), lambda b: (b, 0, 0, 0, 0))
    consts = [cwx, cbx, cwbc, cbbc, a_row, dtb_row, dsk, norm_g[None, :]]
    return pl.pallas_call(
        functools.partial(_ssd_kernel, n_chunks=n_chunks),
        grid=(B,),
        in_specs=[seq(SSD_INNER), seq(SSD_INNER), seq(2 * SSD_BC), seq(DT_PAD)] + [full(a) for a in consts]
        + [st_spec],
        out_specs=[seq(SSD_INNER), st_spec],
        out_shape=[jax.ShapeDtypeStruct((B, L, SSD_INNER), jnp.float32),
                   jax.ShapeDtypeStruct(h0.shape, jnp.float32)],
        scratch_shapes=[pltpu.VMEM((L, SSD_INNER), jnp.float32), pltpu.VMEM((L, 2 * SSD_BC), jnp.float32),
                        pltpu.VMEM((L, DT_PAD), jnp.float32)],
        compiler_params=pltpu.CompilerParams(dimension_semantics=("arbitrary",), vmem_limit_bytes=VMEM_LIMIT_BYTES),
        name="ssd_mixer")(z, xs, bc, dt_raw, *consts, h0)


def _head_rms(x, gain_row, n_heads):
    parts = []
    for h in range(n_heads):
        xh = x[:, h * HEAD_DIM:(h + 1) * HEAD_DIM]
        parts.append(xh * lax.rsqrt(jnp.mean(xh * xh, axis=-1, keepdims=True) + EPS))
    return jnp.concatenate(parts, axis=1) * gain_row


def _rope(x, cos, sin_signed, first16):
    n = x.shape[1]
    partner = jnp.where(first16, pltpu.roll(x, n - 16, 1), pltpu.roll(x, 16, 1))
    return x * cos + partner * sin_signed


def _softmax_rows(s):
    m = jnp.max(s, axis=-1, keepdims=True)
    e = jnp.exp(s - m)
    return e / jnp.sum(e, axis=-1, keepdims=True)


def _dense_attn_kernel(*refs, n_heads, n_kv, has_cache, norm, rope, q_blk):
    it = iter(refs)
    q_ref, k_ref, v_ref = next(it), next(it), next(it)
    ck_ref = cv_ref = qg_ref = kg_ref = cq_ref = sq_ref = ckk_ref = skk_ref = None
    if has_cache:
        ck_ref, cv_ref = next(it), next(it)
    if norm:
        qg_ref, kg_ref = next(it), next(it)
    if rope:
        cq_ref, sq_ref, ckk_ref, skk_ref = next(it), next(it), next(it), next(it)
    o_ref = next(it)
    kn_ref = next(it) if norm else None

    q = q_ref[0]
    k = k_ref[0]
    if norm:
        q = _head_rms(q, qg_ref[...], n_heads)
        k = _head_rms(k, kg_ref[...], n_kv)
        kn_ref[0] = k
    if rope:
        lane_q = lax.broadcasted_iota(jnp.int32, q.shape, 1)
        lane_k = lax.broadcasted_iota(jnp.int32, k.shape, 1)
        q = _rope(q, cq_ref[...], sq_ref[...], lax.bitwise_and(lane_q, 31) < 16)
        k = _rope(k, ckk_ref[...], skk_ref[...], lax.bitwise_and(lane_k, 31) < 16)
    qb = q.astype(jnp.bfloat16)
    kb = k.astype(jnp.bfloat16)
    vb = v_ref[0].astype(jnp.bfloat16)
    if has_cache:
        kb = jnp.concatenate([ck_ref[0].astype(jnp.bfloat16), kb], axis=0)
        vb = jnp.concatenate([cv_ref[0].astype(jnp.bfloat16), vb], axis=0)
    lq = q.shape[0]
    grp = n_heads // n_kv
    scale = HEAD_DIM ** -0.5
    outs = []
    for h in range(n_heads):
        kv = h // grp
        kh = kb[:, kv * HEAD_DIM:(kv + 1) * HEAD_DIM]
        vh = vb[:, kv * HEAD_DIM:(kv + 1) * HEAD_DIM]
        rows = []
        for r0 in range(0, lq, q_blk):
            qh = qb[r0:r0 + q_blk, h * HEAD_DIM:(h + 1) * HEAD_DIM]
            s = lax.dot_general(qh, kh, (((1,), (1,)), ((), ())), preferred_element_type=jnp.float32) * scale
            p = _softmax_rows(s).astype(jnp.bfloat16)
            rows.append(jnp.dot(p, vh, preferred_element_type=jnp.float32))
        outs.append(rows[0] if len(rows) == 1 else jnp.concatenate(rows, axis=0))
    o_ref[0] = jnp.concatenate(outs, axis=1)


def dense_attention(q, k, v, cache_k=None, cache_v=None, q_gain=None, k_gain=None, rope_tabs=None):
    B, L, qw = q.shape
    kw = k.shape[2]
    n_heads, n_kv = qw // HEAD_DIM, kw // HEAD_DIM
    has_cache, norm, rope = cache_k is not None, q_gain is not None, rope_tabs is not None
    args = [q, k, v]
    specs = [pl.BlockSpec((1, L, qw), lambda b: (b, 0, 0)),
             pl.BlockSpec((1, L, kw), lambda b: (b, 0, 0)),
             pl.BlockSpec((1, L, kw), lambda b: (b, 0, 0))]
    if has_cache:
        lc = cache_k.shape[1]
        args += [cache_k, cache_v]
        specs += [pl.BlockSpec((1, lc, kw), lambda b: (b, 0, 0))] * 2
    if norm:
        args += [jnp.tile(q_gain, n_heads)[None, :], jnp.tile(k_gain, n_kv)[None, :]]
        specs += [pl.BlockSpec((1, qw), lambda b: (0, 0)), pl.BlockSpec((1, kw), lambda b: (0, 0))]
    if rope:
        args += list(rope_tabs)
        specs += [pl.BlockSpec((L, qw), lambda b: (0, 0))] * 2 + [pl.BlockSpec((L, kw), lambda b: (0, 0))] * 2
    out_shape = [jax.ShapeDtypeStruct((B, L, qw), jnp.float32)]
    out_specs = [pl.BlockSpec((1, L, qw), lambda b: (b, 0, 0))]
    if norm:
        out_shape.append(jax.ShapeDtypeStruct((B, L, kw), jnp.float32))
        out_specs.append(pl.BlockSpec((1, L, kw), lambda b: (b, 0, 0)))
    kern = functools.partial(_dense_attn_kernel, n_heads=n_heads, n_kv=n_kv, has_cache=has_cache, norm=norm,
                             rope=rope, q_blk=min(L, 256))
    res = pl.pallas_call(kern, grid=(B,), in_specs=specs, out_specs=out_specs, out_shape=out_shape,
                         compiler_params=pltpu.CompilerParams(dimension_semantics=("arbitrary",),
                                                              vmem_limit_bytes=VMEM_LIMIT_BYTES),
                         name="dense_attention")(*args)
    return res if norm else res[0]


def rope_tables(L, n_heads):
    t = np.arange(L)
    pos = np.stack([t // GRID_W, t % GRID_W], axis=1).astype(np.float32)
    half = HEAD_DIM // 2
    inv = 1.0 / (ROPE_THETA ** (jnp.arange(0, half, 2, dtype=jnp.float32) / half))
    ang = jnp.asarray(pos)[:, :, None] * inv[None, None, :]
    cos, sin = jnp.cos(ang), jnp.sin(ang)
    cos64 = jnp.concatenate([cos, cos], axis=2).reshape(L, HEAD_DIM)
    sin64 = jnp.concatenate([-sin, sin], axis=2).reshape(L, HEAD_DIM)
    return jnp.tile(cos64, (1, n_heads)), jnp.tile(sin64, (1, n_heads))


def na_bias_table(rpb):
    qcol = np.arange(GRID_W)[:, None]
    kc = np.arange(GRID_W)[None, :]
    qstart = np.clip(qcol - NA_COLS // 2, 0, GRID_W - NA_COLS)
    valid = (kc >= qstart) & (kc < qstart + NA_COLS)
    dc = np.clip(kc - qcol + NA_COLS - 1, 0, 2 * NA_COLS - 2)
    t = jnp.where(jnp.asarray(valid), rpb[..., dc], NEG_BIG)
    t = jnp.stack([t[..., off:off + NA_ROWS, :, :] for off in range(NA_ROWS)], axis=-4)
    t = jnp.swapaxes(t, -3, -2)
    return t.reshape(t.shape[:-2] + (NA_ROWS * GRID_W,))


def _na_kernel(q_ref, k_ref, v_ref, ck_ref, cv_ref, bias_ref, o_ref, *, rows):
    scale = HEAD_DIM ** -0.5
    qb = q_ref[0].astype(jnp.bfloat16)
    kb = k_ref[0].astype(jnp.bfloat16)
    vb = v_ref[0].astype(jnp.bfloat16)
    ckb = ck_ref[0].astype(jnp.bfloat16)
    cvb = cv_ref[0].astype(jnp.bfloat16)
    n_loc = NA_ROWS * GRID_W
    nt = (((1,), (1,)), ((), ()))
    outs = []
    for hh in range(2):
        sl = slice(hh * HEAD_DIM, (hh + 1) * HEAD_DIM)
        rows_out = []
        for r in range(rows):
            kr0 = min(max(r - NA_ROWS // 2, 0), rows - NA_ROWS)
            off = kr0 - r + NA_ROWS - 1
            qr = qb[r * GRID_W:(r + 1) * GRID_W, sl]
            kl = kb[kr0 * GRID_W:kr0 * GRID_W + n_loc, sl]
            vl = vb[kr0 * GRID_W:kr0 * GRID_W + n_loc, sl]
            s_loc = lax.dot_general(qr, kl, nt, preferred_element_type=jnp.float32) * scale + bias_ref[hh, off]
            s_ctx = lax.dot_general(qr, ckb[:, sl], nt, preferred_element_type=jnp.float32) * scale
            m = jnp.maximum(jnp.max(s_loc, axis=-1, keepdims=True), jnp.max(s_ctx, axis=-1, keepdims=True))
            e_loc = jnp.exp(s_loc - m)
            e_ctx = jnp.exp(s_ctx - m)
            den = jnp.sum(e_loc, axis=-1, keepdims=True) + jnp.sum(e_ctx, axis=-1, keepdims=True)
            p_loc = (e_loc / den).astype(jnp.bfloat16)
            p_ctx = (e_ctx / den).astype(jnp.bfloat16)
            rows_out.append(jnp.dot(p_loc, vl, preferred_element_type=jnp.float32)
                            + jnp.dot(p_ctx, cvb[:, sl], preferred_element_type=jnp.float32))
        outs.append(jnp.concatenate(rows_out, axis=0))
    o_ref[0] = jnp.concatenate(outs, axis=1)


def neighbourhood_attention(q, k, v, ctx_k, ctx_v, bias_tbl):
    B, L, w = q.shape
    lc = ctx_k.shape[1]
    n_pairs = w // LANES
    rows = L // GRID_W
    assert rows >= NA_ROWS
    kern = functools.partial(_na_kernel, rows=rows)
    blk = lambda n: pl.BlockSpec((1, n, LANES), lambda hp, b: (b, 0, hp))
    return pl.pallas_call(
        kern, grid=(n_pairs, B),
        in_specs=[blk(L), blk(L), blk(L), blk(lc), blk(lc),
                  pl.BlockSpec((2, NA_ROWS, GRID_W, NA_ROWS * GRID_W), lambda hp, b: (hp, 0, 0, 0))],
        out_specs=blk(L),
        out_shape=jax.ShapeDtypeStruct((B, L, w), jnp.float32),
        compiler_params=pltpu.CompilerParams(dimension_semantics=("arbitrary", "arbitrary"),
                                             vmem_limit_bytes=VMEM_LIMIT_BYTES),
        name="neighbourhood_attention")(q, k, v, ctx_k, ctx_v, bias_tbl)


def _merge_kernel(ya_ref, yb_ref, yc_ref, gate_ref, x_ref, mod_ref, wb_ref, wo_ref, g_ref, b_ref, wqt_ref,
                  x1_ref, h2_ref, qt_ref):
    m = mod_ref[0]
    merged = None
    for i, y_ref in enumerate((ya_ref, yb_ref, yc_ref)):
        br = jnp.dot(y_ref[...].astype(jnp.bfloat16), wb_ref[i], preferred_element_type=jnp.float32)
        t = _sigmoid(gate_ref[:, i * D_MODEL:(i + 1) * D_MODEL]) * br
        merged = t if merged is None else merged + t
    y = jnp.dot(merged.astype(jnp.bfloat16), wo_ref[...], preferred_element_type=jnp.float32)
    x1 = _ln(DN_ALPHA * x_ref[...] + m[:, 2 * D_MODEL:3 * D_MODEL] * y) * g_ref[...] + b_ref[...]
    x1_ref[...] = x1
    h2 = (_ln(x1) * (1.0 + m[:, 4 * D_MODEL:5 * D_MODEL]) + m[:, 3 * D_MODEL:4 * D_MODEL]).astype(jnp.bfloat16)
    h2_ref[...] = h2
    qt_ref[...] = lax.dot_general(wqt_ref[...], h2, (((1,), (1,)), ((), ())), preferred_element_type=jnp.float32)


def merge_branches(ya, yb, yc, gate, x, mod, w_branch, w_out, ln_g, ln_b, wq_t, is_ctx):
    T, d = x.shape
    nq = wq_t.shape[0]
    row = lambda w: pl.BlockSpec((ROW_TILE, w), lambda i: (i, 0))
    full = lambda a: pl.BlockSpec(a.shape, lambda i: (0,) * a.ndim)
    return pl.pallas_call(
        _merge_kernel, grid=(T // ROW_TILE,),
        in_specs=[row(512), row(512), row(512), row(N_BRANCH * d), row(d),
                  pl.BlockSpec((1, 1, mod.shape[2]), lambda i: (_cond_row(i, is_ctx, ROW_TILE), 0, 0)),
                  full(w_branch), full(w_out), full(ln_g), full(ln_b), full(wq_t)],
        out_specs=[row(d), row(d), pl.BlockSpec((nq, ROW_TILE), lambda i: (0, i))],
        out_shape=[jax.ShapeDtypeStruct((T, d), jnp.float32), jax.ShapeDtypeStruct((T, d), jnp.bfloat16),
                   jax.ShapeDtypeStruct((nq, T), jnp.float32)],
        compiler_params=pltpu.CompilerParams(dimension_semantics=("arbitrary",), vmem_limit_bytes=VMEM_LIMIT_BYTES),
        name="merge_branches")(ya, yb, yc, gate, x, mod, w_branch, w_out, ln_g, ln_b, wq_t)


CAND_GROUPS = ((0, 1, 16), (1, 1, 8), (2, 1, 8), (3, 1, 8), (4, 1, 8), (5, 1, 8), (6, 1, 8), (7, 1, 8), (8, 8, 1))


def _cand_flat_index():
    rows = []
    for a0, na, nb in CAND_GROUPS:
        rows += [a * PEER_TOPK + b for a in range(a0, a0 + na) for b in range(nb)]
    needed = {(a, b) for a in range(PEER_TOPK) for b in range(PEER_TOPK) if (a + 1) * (b + 1) <= PEER_TOPK}
    assert needed <= {(r // PEER_TOPK, r % PEER_TOPK) for r in rows}
    return np.asarray(rows, np.float32)


def _topk_rows(s, ids, k):
    vals, idxs = [], []
    big = jnp.float32(1e9)
    for _ in range(k):
        m = jnp.max(s, axis=0, keepdims=True)
        idx = jnp.min(jnp.where(s == m, ids, big), axis=0, keepdims=True)
        vals.append(m)
        idxs.append(idx)
        s = jnp.where(ids == idx, -jnp.inf, s)
    return jnp.concatenate(vals, axis=0), jnp.concatenate(idxs, axis=0)


def _select_rows(table, sel):
    out = jnp.zeros(sel.shape, table.dtype)
    for a in range(table.shape[0]):
        out = jnp.where(sel == float(a), table[a:a + 1, :], out)
    return out


def _peer_topk_kernel(qt_ref, keys_ref, cid_ref, et_ref, gt_ref):
    key_ids = lax.broadcasted_iota(jnp.int32, (PEER_KEYS, TOPK_TOK), 0).astype(jnp.float32)

    def head(h, carry):
        sv, si = [], []
        for p in range(2):
            q = qt_ref[pl.ds(pl.multiple_of(h * 256 + p * 128, 128), 128), :].astype(jnp.bfloat16)
            s = jnp.dot(keys_ref[h, p], q, preferred_element_type=jnp.float32)
            v, i = _topk_rows(s, key_ids, PEER_TOPK)
            sv.append(v)
            si.append(i)
        groups = []
        for a0, na, nb in CAND_GROUPS:
            groups.append(sv[0][a0:a0 + na, :] + sv[1][0:nb, :])
        cand = jnp.concatenate(groups, axis=0)
        cv, ci = _topk_rows(cand, cid_ref[...], PEER_TOPK)
        a_sel = jnp.floor(ci * (1.0 / PEER_TOPK))
        b_sel = ci - a_sel * PEER_TOPK
        i_sel = _select_rows(si[0], a_sel)
        j_sel = _select_rows(si[1], b_sel)
        ex = jnp.exp(cv - cv[0:1, :])
        g = ex / jnp.sum(ex, axis=0, keepdims=True)
        row = pl.multiple_of(h * PEER_TOPK, PEER_TOPK)
        et_ref[pl.ds(row, PEER_TOPK), :] = (i_sel * PEER_KEYS + j_sel).astype(jnp.int32)
        gt_ref[pl.ds(row, PEER_TOPK), :] = g
        return carry

    lax.fori_loop(0, PEER_HEADS, head, 0)


def peer_topk(qt, keys_bf16):
    T = qt.shape[1]
    cid = jnp.asarray(np.tile(_cand_flat_index()[:, None], (1, TOPK_TOK)))
    return pl.pallas_call(
        _peer_topk_kernel,
        grid=(T // TOPK_TOK,),
        in_specs=[pl.BlockSpec((qt.shape[0], TOPK_TOK), lambda i: (0, i)),
                  pl.BlockSpec(keys_bf16.shape, lambda i: (0, 0, 0, 0)),
                  pl.BlockSpec(cid.shape, lambda i: (0, 0))],
        out_specs=[pl.BlockSpec((PEER_HEADS * PEER_TOPK, TOPK_TOK), lambda i: (0, i)),
                   pl.BlockSpec((PEER_HEADS * PEER_TOPK, TOPK_TOK), lambda i: (0, i))],
        out_shape=[jax.ShapeDtypeStruct((PEER_HEADS * PEER_TOPK, T), jnp.int32),
                   jax.ShapeDtypeStruct((PEER_HEADS * PEER_TOPK, T), jnp.float32)],
        compiler_params=pltpu.CompilerParams(dimension_semantics=("arbitrary",)),
        name="peer_topk",
    )(qt, keys_bf16, cid)


def _peer_dense_kernel(x_ref, e_ref, g_ref, ut_ref, v_ref, x1_ref, mod_ref, lg_ref, lb_ref, o_ref, g3_ref, acc_ref,
                       *, t_blk, te):
    j = pl.program_id(1)
    nk = PEER_KEYS
    n_i = te // nk

    @pl.when(j == 0)
    def _build():
        acc_ref[...] = jnp.zeros_like(acc_ref)
        sub_iota = lax.broadcasted_iota(jnp.int32, (nk, nk), 0)
        zeros = jnp.zeros((nk, nk), jnp.bfloat16)

        def body(tp, carry):
            onehot_i, gated_j = [], []
            for u in range(2):
                e_row = e_ref[pl.ds(2 * tp + u, 1), :]
                g_row = g_ref[pl.ds(2 * tp + u, 1), :]
                i_row = lax.shift_right_logical(e_row, 7)
                j_row = lax.bitwise_and(e_row, nk - 1)
                onehot_i.append(jnp.where(sub_iota == i_row, 1.0, 0.0).astype(jnp.bfloat16))
                gated_j.append(jnp.where(sub_iota == j_row, g_row, 0.0).astype(jnp.bfloat16))
            lhs = jnp.concatenate(onehot_i, axis=1)
            rhs = jnp.concatenate([jnp.concatenate([gated_j[0], zeros], axis=1),
                                   jnp.concatenate([zeros, gated_j[1]], axis=1)], axis=0)
            gt = lax.dot_general(lhs, rhs, (((1,), (1,)), ((), ())), preferred_element_type=jnp.float32)
            for u in range(2):
                row0 = pl.multiple_of((2 * tp + u) * G3_PITCH, 8)
                g3_ref[pl.ds(row0, nk), :] = gt[:, u * nk:(u + 1) * nk]
            return carry

        lax.fori_loop(0, t_blk // 2, body, 0, unroll=4)

    s = jnp.dot(x_ref[...], ut_ref[...], preferred_element_type=jnp.float32)
    parts = []
    for k in range(n_i):
        gi = g3_ref[pl.ds(j * n_i + k, t_blk, stride=G3_PITCH), :]
        parts.append((_gelu_tanh(s[:, k * nk:(k + 1) * nk]) * gi).astype(jnp.bfloat16))
    a = jnp.concatenate(parts, axis=1)
    acc_ref[...] += jnp.dot(a, v_ref[...], preferred_element_type=jnp.float32)

    @pl.when(j == pl.num_programs(1) - 1)
    def _fin():
        gate2 = mod_ref[0][:, 5 * D_MODEL:6 * D_MODEL]
        o_ref[...] = _ln(DN_ALPHA * x1_ref[...] + gate2 * acc_ref[...]) * lg_ref[...] + lb_ref[...]


def peer_dense(x_bf16, e, g, ut_bf16, v_bf16, x1, mod, ln_g, ln_b, is_ctx):
    T, D = x_bf16.shape
    NE = v_bf16.shape[0]
    t_blk, te = PEER_TOK_BLOCK, PEER_EXPERT_TILE
    kern = functools.partial(_peer_dense_kernel, t_blk=t_blk, te=te)
    tok = lambda w: pl.BlockSpec((t_blk, w), lambda i, j: (i, 0))
    return pl.pallas_call(
        kern,
        grid=(T // t_blk, NE // te),
        in_specs=[tok(D), tok(LANES), tok(LANES),
                  pl.BlockSpec((D, te), lambda i, j: (0, j)),
                  pl.BlockSpec((te, D), lambda i, j: (j, 0)),
                  tok(D),
                  pl.BlockSpec((1, 1, mod.shape[2]), lambda i, j: (_cond_row(i, is_ctx, t_blk), 0, 0)),
                  pl.BlockSpec((1, D), lambda i, j: (0, 0)),
                  pl.BlockSpec((1, D), lambda i, j: (0, 0))],
        out_specs=tok(D),
        out_shape=jax.ShapeDtypeStruct((T, D), jnp.float32),
        scratch_shapes=[pltpu.VMEM((t_blk * G3_PITCH, PEER_KEYS), jnp.float32),
                        pltpu.VMEM((t_blk, D), jnp.float32)],
        compiler_params=pltpu.CompilerParams(dimension_semantics=("arbitrary", "arbitrary"),
                                             vmem_limit_bytes=VMEM_LIMIT_BYTES),
        name="peer_dense",
    )(x_bf16, e, g, ut_bf16, v_bf16, x1, mod, ln_g, ln_b)


def _in_proj_weight(w_in_l):
    offs = np.cumsum((0,) + IN_SIZES)
    col = lambda i: w_in_l[:, offs[i]:offs[i + 1]]
    dt_pad = jnp.pad(col(4), ((0, 0), (0, DT_PAD - IN_SIZES[4])))
    parts = [col(0), col(1), col(2), col(3), dt_pad] + [col(i) for i in range(5, 12)]
    return jnp.concatenate(parts, axis=1).astype(jnp.bfloat16)


def kernel(x_prompt, x_sample, cache_na_k, cache_na_v, cache_gqa_k, cache_gqa_v, state_ssd, c, c_ctx,
           w_mod, b_mod, w_in, conv_w, conv_b, ssd_a_log, ssd_dt_bias, ssd_d, ssd_norm_g, na_rpb,
           gqa_q_norm, gqa_k_norm, w_branch, w_out, ln1_g, ln1_b, ln2_g, ln2_b,
           peer_wq, peer_keys, peer_u, peer_v):
    bf = jnp.bfloat16
    cond = jnp.zeros((COND_ROWS, D_MODEL), jnp.float32).at[0].set(c_ctx).at[1:1 + DEC_BATCH].set(c)
    mod_all = adaln_table(cond, w_mod, b_mod)
    bias_all = na_bias_table(na_rpb)
    rope_q = rope_tables(DEC_SEQ, GQA_HEADS)
    rope_k = rope_tables(DEC_SEQ, GQA_KV_HEADS)
    zero_state = jnp.zeros((BATCH, 2, SSD_HEADS, SSD_HEAD_DIM, SSD_STATE), jnp.float32)

    streams = {True: x_prompt.reshape(BATCH * SEQ, D_MODEL), False: x_sample.reshape(DEC_BATCH * DEC_SEQ, D_MODEL)}
    new_na_k, new_na_v, new_g_k, new_g_v, new_ssd = [], [], [], [], []
    for l in range(DEPTH):
        mod = mod_all[l].reshape(COND_ROWS, 1, 6 * D_MODEL)
        w_cat = _in_proj_weight(w_in[l])
        wb, wo, wq_t = w_branch[l].astype(bf), w_out[l].astype(bf), peer_wq[l].T.astype(bf)
        keys_b = peer_keys[l].astype(bf)
        ut, vb = peer_u[l].T.astype(bf), peer_v[l].astype(bf)
        bias_tbl = bias_all[l]
        ssd_args = (conv_w[l], conv_b[l], ssd_a_log[l], ssd_dt_bias[l], ssd_d[l], ssd_norm_g[l])
        for is_ctx in (True, False):
            x = streams[is_ctx]
            nb, sl = (BATCH, SEQ) if is_ctx else (DEC_BATCH, DEC_SEQ)
            z, xs, bc, dt, naq, nak, nav, gq, gk, gv, gate = in_proj(x, mod, w_cat, is_ctx)
            r3 = lambda a: a.reshape(nb, sl, a.shape[-1])
            if is_ctx:
                ya, st = ssd_mixer(r3(z), r3(xs), r3(bc), r3(dt), *ssd_args, zero_state)
                yb = dense_attention(r3(naq), r3(nak), r3(nav))
                yc, gk_n = dense_attention(r3(gq), r3(gk), r3(gv), q_gain=gqa_q_norm[l], k_gain=gqa_k_norm[l])
                new_na_k.append(nak.reshape(BATCH, SEQ, NA_HEADS, HEAD_DIM))
                new_na_v.append(nav.reshape(BATCH, SEQ, NA_HEADS, HEAD_DIM))
                new_g_k.append(gk_n.reshape(BATCH, SEQ, GQA_KV_HEADS, HEAD_DIM))
                new_g_v.append(gv.reshape(BATCH, SEQ, GQA_KV_HEADS, HEAD_DIM))
                new_ssd.append(st)
            else:
                ya, _ = ssd_mixer(r3(z), r3(xs), r3(bc), r3(dt), *ssd_args, state_ssd[:, l])
                yb = neighbourhood_attention(r3(naq), r3(nak), r3(nav),
                                             cache_na_k[:, l].reshape(DEC_BATCH, PAST_LEN, NA_HEADS * HEAD_DIM),
                                             cache_na_v[:, l].reshape(DEC_BATCH, PAST_LEN, NA_HEADS * HEAD_DIM),
                                             bias_tbl)
                yc, _ = dense_attention(r3(gq), r3(gk), r3(gv),
                                        cache_gqa_k[:, l].reshape(DEC_BATCH, PAST_LEN, GQA_KV_HEADS * HEAD_DIM),
                                        cache_gqa_v[:, l].reshape(DEC_BATCH, PAST_LEN, GQA_KV_HEADS * HEAD_DIM),
                                        gqa_q_norm[l], gqa_k_norm[l], rope_q + rope_k)
            r2 = lambda a: a.reshape(nb * sl, a.shape[-1])
            x1, h2, qt = merge_branches(r2(ya), r2(yb), r2(yc), gate, x, mod, wb, wo,
                                        ln1_g[l][None, :], ln1_b[l][None, :], wq_t, is_ctx)
            et, gt = peer_topk(qt, keys_b)
            streams[is_ctx] = peer_dense(h2, et.T, gt.T, ut, vb, x1, mod,
                                         ln2_g[l][None, :], ln2_b[l][None, :], is_ctx)

    return (streams[True].reshape(BATCH, SEQ, D_MODEL), streams[False].reshape(DEC_BATCH, DEC_SEQ, D_MODEL),
            jnp.stack(new_na_k, axis=1), jnp.stack(new_na_v, axis=1),
            jnp.stack(new_g_k, axis=1), jnp.stack(new_g_v, axis=1), jnp.stack(new_ssd, axis=1))
```

```python
import functools

import jax
import jax.numpy as jnp
import numpy as np
from jax import lax
from jax.experimental import pallas as pl
from jax.experimental.pallas import tpu as pltpu

D_MODEL = 1024
BATCH = 16
SEQ = 256
DEPTH = 4
DEC_BATCH = 4
DEC_SEQ = 1024
PAST_LEN = 256
GRID_W = 64
HEAD_DIM = 64
N_BRANCH = 3
SSD_HEADS = 8
SSD_HEAD_DIM = 64
SSD_INNER = SSD_HEADS * SSD_HEAD_DIM
SSD_GROUPS = 2
SSD_STATE = 64
SSD_BC = SSD_GROUPS * SSD_STATE
SSD_CHUNK = 128
CONV_W = 5
NA_HEADS = 8
NA_ROWS = 8
NA_COLS = 16
GQA_HEADS = 8
GQA_KV_HEADS = 2
ROPE_THETA = 10000.0
PEER_HEADS = 8
PEER_KEYS = 128
PEER_TOPK = 16
DN_ALPHA = (2 * DEPTH) ** 0.25
EPS = 1e-6
IN_SIZES = (SSD_INNER, SSD_INNER, SSD_BC, SSD_BC, 2 * SSD_HEADS,
            NA_HEADS * HEAD_DIM, NA_HEADS * HEAD_DIM, NA_HEADS * HEAD_DIM,
            GQA_HEADS * HEAD_DIM, GQA_KV_HEADS * HEAD_DIM, GQA_KV_HEADS * HEAD_DIM,
            N_BRANCH * D_MODEL)

LANES = 128
VMEM_LIMIT_BYTES = 56 * 1024 * 1024
NEG_BIG = -1e30
ROW_TILE = 256
COND_ROWS = 8
DT_PAD = LANES
IN_OUT_WIDTHS = (SSD_INNER, SSD_INNER, 2 * SSD_BC, DT_PAD, 512, 512, 512, 512, 128, 128, N_BRANCH * D_MODEL)
PEER_TOK_BLOCK = 512
PEER_EXPERT_TILE = 512
G3_PITCH = 136
TOPK_TOK = LANES
NA_SOFTMAX_ROWS = 128


def _cond_row(i, is_ctx, rows_per_step):
    if is_ctx:
        return 0
    return 1 + i // (DEC_SEQ // rows_per_step)


def _ln(x):
    mu = jnp.mean(x, axis=-1, keepdims=True)
    xc = x - mu
    var = jnp.mean(xc * xc, axis=-1, keepdims=True)
    return xc * lax.rsqrt(var + EPS)


def _silu(x):
    return x / (1.0 + jnp.exp(-x))


def _sigmoid(x):
    return 1.0 / (1.0 + jnp.exp(-x))


def _gelu_tanh(x):
    return 0.5 * x * (1.0 + jnp.tanh(0.7978845608028654 * (x + 0.044715 * (x * x * x))))


def _adaln_kernel(c_ref, w_ref, b_ref, o_ref):
    a = _silu(c_ref[...]).astype(jnp.bfloat16)
    o_ref[0] = jnp.dot(a, w_ref[0].astype(jnp.bfloat16), preferred_element_type=jnp.float32) + b_ref[0]


def adaln_table(cond, w_mod, b_mod, tn=1536):
    depth, d, n = w_mod.shape
    return pl.pallas_call(
        _adaln_kernel, grid=(depth, n // tn),
        in_specs=[pl.BlockSpec((COND_ROWS, d), lambda l, j: (0, 0)),
                  pl.BlockSpec((1, d, tn), lambda l, j: (l, 0, j)),
                  pl.BlockSpec((1, 1, tn), lambda l, j: (l, 0, j))],
        out_specs=pl.BlockSpec((1, COND_ROWS, tn), lambda l, j: (l, 0, j)),
        out_shape=jax.ShapeDtypeStruct((depth, COND_ROWS, n), jnp.float32),
        compiler_params=pltpu.CompilerParams(dimension_semantics=("arbitrary", "arbitrary"),
                                             vmem_limit_bytes=VMEM_LIMIT_BYTES),
        name="adaln_table")(cond, w_mod, b_mod.reshape(depth, 1, n))


def _in_proj_kernel(x_ref, mod_ref, w_ref, *o_refs):
    m = mod_ref[0]
    h = _ln(x_ref[...]) * (1.0 + m[:, D_MODEL:2 * D_MODEL]) + m[:, 0:D_MODEL]
    p = jnp.dot(h.astype(jnp.bfloat16), w_ref[...], preferred_element_type=jnp.float32)
    off = 0
    for o_ref, w in zip(o_refs, IN_OUT_WIDTHS):
        o_ref[...] = p[:, off:off + w]
        off += w


def in_proj(x, mod, w_cat, is_ctx):
    T, d = x.shape
    n = w_cat.shape[1]
    return pl.pallas_call(
        _in_proj_kernel, grid=(T // ROW_TILE,),
        in_specs=[pl.BlockSpec((ROW_TILE, d), lambda i: (i, 0)),
                  pl.BlockSpec((1, 1, mod.shape[2]), lambda i: (_cond_row(i, is_ctx, ROW_TILE), 0, 0)),
                  pl.BlockSpec((d, n), lambda i: (0, 0), pipeline_mode=pl.Buffered(1))],
        out_specs=[pl.BlockSpec((ROW_TILE, w), lambda i: (i, 0)) for w in IN_OUT_WIDTHS],
        out_shape=[jax.ShapeDtypeStruct((T, w), jnp.float32) for w in IN_OUT_WIDTHS],
        compiler_params=pltpu.CompilerParams(dimension_semantics=("arbitrary",), vmem_limit_bytes=VMEM_LIMIT_BYTES),
        name="in_proj")(x, mod, w_cat)


def _dwconv_silu(x, w_ref, b_ref):
    L = x.shape[0]
    row = lax.broadcasted_iota(jnp.int32, x.shape, 0)
    acc = x * w_ref[CONV_W // 2:CONV_W // 2 + 1, :] + b_ref[...]
    for w in range(CONV_W):
        s = w - CONV_W // 2
        if s == 0:
            continue
        shifted = pltpu.roll(x, (-s) % L, 0)
        ok = (row + s >= 0) & (row + s < L)
        acc = acc + jnp.where(ok, shifted, 0.0) * w_ref[w:w + 1, :]
    return _silu(acc)


def _cumsum_rows(a, reverse):
    q = a.shape[0]
    row = lax.broadcasted_iota(jnp.int32, a.shape, 0)
    s = 1
    while s < q:
        if reverse:
            a = a + jnp.where(row < q - s, pltpu.roll(a, q - s, 0), 0.0)
        else:
            a = a + jnp.where(row >= s, pltpu.roll(a, s, 0), 0.0)
        s *= 2
    return a


def _ssd_kernel(z_ref, xs_ref, bc_ref, dt_ref, cwx_ref, cbx_ref, cwbc_ref, cbbc_ref, a_ref, dtb_ref, dsk_ref,
                ng_ref, h0_ref, y_ref, st_ref, xc_ref, bcc_ref, dtp_ref, *, n_chunks):
    Q = SSD_CHUNK
    P = SSD_HEAD_DIM
    N = SSD_STATE
    rep = SSD_HEADS // SSD_GROUPS
    xc_ref[...] = _dwconv_silu(xs_ref[0], cwx_ref, cbx_ref)
    bcc_ref[...] = _dwconv_silu(bc_ref[0], cwbc_ref, cbbc_ref)
    pre = dt_ref[0] + dtb_ref[...]
    dtp_ref[...] = jnp.maximum(pre, 0.0) + jnp.log1p(jnp.exp(-jnp.abs(pre)))
    y_ref[0] = xc_ref[...] * dsk_ref[...]
    st_ref[0] = h0_ref[0]
    ii = lax.broadcasted_iota(jnp.int32, (Q, Q), 0)
    jj = lax.broadcasted_iota(jnp.int32, (Q, Q), 1)
    nt = (((1,), (1,)), ((), ()))

    def chunk_pair(c, carry):
        for d in range(2):
            cc = c if d == 0 else n_chunks - 1 - c
            r0 = pl.multiple_of(cc * Q, Q)
            x = xc_ref[pl.ds(r0, Q), :]
            bcv = bcc_ref[pl.ds(r0, Q), :]
            dt = dtp_ref[pl.ds(r0, Q), :]
            a = dt * a_ref[...]
            cs = _cumsum_rows(a, reverse=(d == 1))
            tot = cs[Q - 1:Q, :] if d == 0 else cs[0:1, :]
            dec = jnp.exp(tot - cs)
            inn = jnp.exp(cs)
            cs_t = cs.T
            w_t = (dt * dec).T
            etot = jnp.exp(tot)
            x_t = x.T
            keep = (jj <= ii) if d == 0 else (jj >= ii)
            ys = []
            for g in range(SSD_GROUPS):
                bg = bcv[:, g * N:(g + 1) * N].astype(jnp.bfloat16)
                cg = bcv[:, SSD_BC + g * N:SSD_BC + (g + 1) * N].astype(jnp.bfloat16)
                cb = lax.dot_general(cg, bg, nt, preferred_element_type=jnp.float32)
                for hg in range(rep):
                    h = g * rep + hg
                    ln = d * SSD_HEADS + h
                    seg = cs[:, ln:ln + 1] - cs_t[ln:ln + 1, :]
                    lmat = jnp.where(keep, jnp.exp(jnp.where(keep, seg, 0.0)), 0.0)
                    xh = x[:, h * P:(h + 1) * P]
                    xd = (xh * dt[:, ln:ln + 1]).astype(jnp.bfloat16)
                    y_diag = jnp.dot((cb * lmat).astype(jnp.bfloat16), xd, preferred_element_type=jnp.float32)
                    hprev = st_ref[0, d, h]
                    y_off = lax.dot_general(cg, hprev.astype(jnp.bfloat16), nt,
                                            preferred_element_type=jnp.float32) * inn[:, ln:ln + 1]
                    ys.append(y_diag + y_off)
                    xw = (x_t[h * P:(h + 1) * P, :] * w_t[ln:ln + 1, :]).astype(jnp.bfloat16)
                    s_new = jnp.dot(xw, bg, preferred_element_type=jnp.float32)
                    st_ref[0, d, h] = hprev * etot[:, ln:ln + 1] + s_new
            y_ref[0, pl.ds(r0, Q), :] += jnp.concatenate(ys, axis=1)
        return carry

    lax.fori_loop(0, n_chunks, chunk_pair, 0)
    y = y_ref[0] * _silu(z_ref[0])
    y = y * lax.rsqrt(jnp.mean(y * y, axis=-1, keepdims=True) + EPS)
    y_ref[0] = y * ng_ref[...]


def ssd_mixer(z, xs, bc, dt_raw, conv_w, conv_b, a_log, dt_bias, d_skip, norm_g, h0, layer=None):
    B, L, _ = z.shape
    n_chunks = L // SSD_CHUNK
    pad = lambda v: jnp.pad(v.reshape(1, -1), ((0, 0), (0, LANES - v.size)))
    a_row = pad(-jnp.exp(a_log))
    dtb_row = pad(dt_bias)
    dsk = jnp.repeat(d_skip, SSD_HEAD_DIM)[None, :]
    seq = lambda w: pl.BlockSpec((1, L, w), lambda b: (b, 0, 0))
    full = lambda a: pl.BlockSpec(a.shape, lambda b: (0,) * a.ndim)
    cwx, cwbc = conv_w[:, :SSD_INNER], conv_w[:, SSD_INNER:]
    cbx, cbbc = conv_b[None, :SSD_INNER], conv_b[None, SSD_INNER:]
    st_shape = (2, SSD_HEADS, SSD_HEAD_DIM, SSD_STATE)
    st_spec = pl.BlockSpec((1,) + st_shape, lambda b: (b, 0, 0, 0, 0))
    h0_spec = st_spec if layer is None else pl.BlockSpec((1, None) + st_shape, lambda b: (b, layer, 0, 0, 0, 0))
    consts = [cwx, cbx, cwbc, cbbc, a_row, dtb_row, dsk, norm_g[None, :]]
    return pl.pallas_call(
        functools.partial(_ssd_kernel, n_chunks=n_chunks),
        grid=(B,),
        in_specs=[seq(SSD_INNER), seq(SSD_INNER), seq(2 * SSD_BC), seq(DT_PAD)] + [full(a) for a in consts]
        + [h0_spec],
        out_specs=[seq(SSD_INNER), st_spec],
        out_shape=[jax.ShapeDtypeStruct((B, L, SSD_INNER), jnp.float32),
                   jax.ShapeDtypeStruct((B,) + st_shape, jnp.float32)],
        scratch_shapes=[pltpu.VMEM((L, SSD_INNER), jnp.float32), pltpu.VMEM((L, 2 * SSD_BC), jnp.float32),
                        pltpu.VMEM((L, DT_PAD), jnp.float32)],
        compiler_params=pltpu.CompilerParams(dimension_semantics=("arbitrary",), vmem_limit_bytes=VMEM_LIMIT_BYTES),
        name="ssd_mixer")(z, xs, bc, dt_raw, *consts, h0)


def _head_rms(x, gain_row, n_heads):
    parts = []
    for h in range(n_heads):
        xh = x[:, h * HEAD_DIM:(h + 1) * HEAD_DIM]
        parts.append(xh * lax.rsqrt(jnp.mean(xh * xh, axis=-1, keepdims=True) + EPS))
    return jnp.concatenate(parts, axis=1) * gain_row


def _rope(x, cos, sin_signed, first16):
    n = x.shape[1]
    partner = jnp.where(first16, pltpu.roll(x, n - 16, 1), pltpu.roll(x, 16, 1))
    return x * cos + partner * sin_signed


def _softmax_rows(s):
    m = jnp.max(s, axis=-1, keepdims=True)
    e = jnp.exp(s - m)
    return e / jnp.sum(e, axis=-1, keepdims=True)


def _dense_attn_kernel(*refs, n_heads, n_kv, has_cache, norm, rope, q_blk):
    it = iter(refs)
    q_ref, k_ref, v_ref = next(it), next(it), next(it)
    ck_ref = cv_ref = qg_ref = kg_ref = cq_ref = sq_ref = ckk_ref = skk_ref = None
    if has_cache:
        ck_ref, cv_ref = next(it), next(it)
    if norm:
        qg_ref, kg_ref = next(it), next(it)
    if rope:
        cq_ref, sq_ref, ckk_ref, skk_ref = next(it), next(it), next(it), next(it)
    o_ref = next(it)
    kn_ref = next(it) if norm else None

    q = q_ref[0]
    k = k_ref[0]
    if norm:
        q = _head_rms(q, qg_ref[...], n_heads)
        k = _head_rms(k, kg_ref[...], n_kv)
        kn_ref[0] = k
    if rope:
        lane_q = lax.broadcasted_iota(jnp.int32, q.shape, 1)
        lane_k = lax.broadcasted_iota(jnp.int32, k.shape, 1)
        q = _rope(q, cq_ref[...], sq_ref[...], lax.bitwise_and(lane_q, 31) < 16)
        k = _rope(k, ckk_ref[...], skk_ref[...], lax.bitwise_and(lane_k, 31) < 16)
    qb = q.astype(jnp.bfloat16)
    kb = k.astype(jnp.bfloat16)
    vb = v_ref[0].astype(jnp.bfloat16)
    if has_cache:
        kb = jnp.concatenate([ck_ref[0].astype(jnp.bfloat16), kb], axis=0)
        vb = jnp.concatenate([cv_ref[0].astype(jnp.bfloat16), vb], axis=0)
    lq = q.shape[0]
    grp = n_heads // n_kv
    scale = HEAD_DIM ** -0.5
    outs = []
    for h in range(n_heads):
        kv = h // grp
        kh = kb[:, kv * HEAD_DIM:(kv + 1) * HEAD_DIM]
        vh = vb[:, kv * HEAD_DIM:(kv + 1) * HEAD_DIM]
        rows = []
        for r0 in range(0, lq, q_blk):
            qh = qb[r0:r0 + q_blk, h * HEAD_DIM:(h + 1) * HEAD_DIM]
            s = lax.dot_general(qh, kh, (((1,), (1,)), ((), ())), preferred_element_type=jnp.float32) * scale
            p = _softmax_rows(s).astype(jnp.bfloat16)
            rows.append(jnp.dot(p, vh, preferred_element_type=jnp.float32))
        outs.append(rows[0] if len(rows) == 1 else jnp.concatenate(rows, axis=0))
    o_ref[0] = jnp.concatenate(outs, axis=1)


def dense_attention(q, k, v, cache_k=None, cache_v=None, q_gain=None, k_gain=None, rope_tabs=None, layer=0):
    B, L, qw = q.shape
    kw = k.shape[2]
    n_heads, n_kv = qw // HEAD_DIM, kw // HEAD_DIM
    has_cache, norm, rope = cache_k is not None, q_gain is not None, rope_tabs is not None
    args = [q, k, v]
    specs = [pl.BlockSpec((1, L, qw), lambda b: (b, 0, 0)),
             pl.BlockSpec((1, L, kw), lambda b: (b, 0, 0)),
             pl.BlockSpec((1, L, kw), lambda b: (b, 0, 0))]
    if has_cache:
        lc = cache_k.shape[2]
        args += [cache_k, cache_v]
        specs += [pl.BlockSpec((1, None, lc, kw), lambda b: (b, layer, 0, 0))] * 2
    if norm:
        args += [jnp.tile(q_gain, n_heads)[None, :], jnp.tile(k_gain, n_kv)[None, :]]
        specs += [pl.BlockSpec((1, qw), lambda b: (0, 0)), pl.BlockSpec((1, kw), lambda b: (0, 0))]
    if rope:
        args += list(rope_tabs)
        specs += [pl.BlockSpec((L, qw), lambda b: (0, 0))] * 2 + [pl.BlockSpec((L, kw), lambda b: (0, 0))] * 2
    out_shape = [jax.ShapeDtypeStruct((B, L, qw), jnp.float32)]
    out_specs = [pl.BlockSpec((1, L, qw), lambda b: (b, 0, 0))]
    if norm:
        out_shape.append(jax.ShapeDtypeStruct((B, L, kw), jnp.float32))
        out_specs.append(pl.BlockSpec((1, L, kw), lambda b: (b, 0, 0)))
    kern = functools.partial(_dense_attn_kernel, n_heads=n_heads, n_kv=n_kv, has_cache=has_cache, norm=norm,
                             rope=rope, q_blk=min(L, 256))
    res = pl.pallas_call(kern, grid=(B,), in_specs=specs, out_specs=out_specs, out_shape=out_shape,
                         compiler_params=pltpu.CompilerParams(dimension_semantics=("arbitrary",),
                                                              vmem_limit_bytes=VMEM_LIMIT_BYTES),
                         name="dense_attention")(*args)
    return res if norm else res[0]


def rope_tables(L, n_heads):
    t = np.arange(L)
    pos = np.stack([t // GRID_W, t % GRID_W], axis=1).astype(np.float32)
    half = HEAD_DIM // 2
    inv = 1.0 / (ROPE_THETA ** (jnp.arange(0, half, 2, dtype=jnp.float32) / half))
    ang = jnp.asarray(pos)[:, :, None] * inv[None, None, :]
    cos, sin = jnp.cos(ang), jnp.sin(ang)
    cos64 = jnp.concatenate([cos, cos], axis=2).reshape(L, HEAD_DIM)
    sin64 = jnp.concatenate([-sin, sin], axis=2).reshape(L, HEAD_DIM)
    return jnp.tile(cos64, (1, n_heads)), jnp.tile(sin64, (1, n_heads))


def na_bias_table(rpb):
    qcol = np.arange(GRID_W)[:, None]
    kc = np.arange(GRID_W)[None, :]
    qstart = np.clip(qcol - NA_COLS // 2, 0, GRID_W - NA_COLS)
    valid = (kc >= qstart) & (kc < qstart + NA_COLS)
    dc = np.clip(kc - qcol + NA_COLS - 1, 0, 2 * NA_COLS - 2)
    t = jnp.where(jnp.asarray(valid), rpb[..., dc], NEG_BIG)
    t = jnp.stack([t[..., off:off + NA_ROWS, :, :] for off in range(NA_ROWS)], axis=-4)
    t = jnp.swapaxes(t, -3, -2)
    return t.reshape(t.shape[:-2] + (NA_ROWS * GRID_W,))


def _na_kernel(q_ref, k_ref, v_ref, ck_ref, cv_ref, bias_ref, o_ref, s_ref, p_ref, *, rows):
    scale = HEAD_DIM ** -0.5
    qb = q_ref[0].astype(jnp.bfloat16)
    kb = k_ref[0].astype(jnp.bfloat16)
    vb = v_ref[0].astype(jnp.bfloat16)
    ckb = ck_ref[0].astype(jnp.bfloat16)
    cvb = cv_ref[0].astype(jnp.bfloat16)
    n_loc = NA_ROWS * GRID_W
    L = qb.shape[0]
    nt = (((1,), (1,)), ((), ()))
    key_row0 = [min(max(r - NA_ROWS // 2, 0), rows - NA_ROWS) for r in range(rows)]
    outs = []
    for hh in range(2):
        sl = slice(hh * HEAD_DIM, (hh + 1) * HEAD_DIM)
        qh = qb[:, sl]
        s_ref[:, n_loc:] = lax.dot_general(qh, ckb[:, sl], nt, preferred_element_type=jnp.float32) * scale
        for r, kr0 in enumerate(key_row0):
            kl = kb[kr0 * GRID_W:kr0 * GRID_W + n_loc, sl]
            s_loc = lax.dot_general(qh[r * GRID_W:(r + 1) * GRID_W], kl, nt, preferred_element_type=jnp.float32)
            s_ref[r * GRID_W:(r + 1) * GRID_W, :n_loc] = s_loc * scale + bias_ref[hh, kr0 - r + NA_ROWS - 1]
        for r0 in range(0, L, NA_SOFTMAX_ROWS):
            p_ref[r0:r0 + NA_SOFTMAX_ROWS, :] = _softmax_rows(s_ref[r0:r0 + NA_SOFTMAX_ROWS, :]).astype(jnp.bfloat16)
        o_ctx = jnp.dot(p_ref[:, n_loc:], cvb[:, sl], preferred_element_type=jnp.float32)
        o_loc = []
        for r, kr0 in enumerate(key_row0):
            vl = vb[kr0 * GRID_W:kr0 * GRID_W + n_loc, sl]
            o_loc.append(jnp.dot(p_ref[r * GRID_W:(r + 1) * GRID_W, :n_loc], vl, preferred_element_type=jnp.float32))
        outs.append(o_ctx + jnp.concatenate(o_loc, axis=0))
    o_ref[0] = jnp.concatenate(outs, axis=1)


def neighbourhood_attention(q, k, v, ctx_k, ctx_v, bias_tbl, layer):
    B, L, w = q.shape
    lc = ctx_k.shape[2]
    n_pairs = w // LANES
    rows = L // GRID_W
    assert rows >= NA_ROWS
    kern = functools.partial(_na_kernel, rows=rows)
    blk = lambda n: pl.BlockSpec((1, n, LANES), lambda hp, b: (b, 0, hp))
    cache_blk = pl.BlockSpec((1, None, lc, LANES), lambda hp, b: (b, layer, 0, hp))
    return pl.pallas_call(
        kern, grid=(n_pairs, B),
        in_specs=[blk(L), blk(L), blk(L), cache_blk, cache_blk,
                  pl.BlockSpec((2, NA_ROWS, GRID_W, NA_ROWS * GRID_W), lambda hp, b: (hp, 0, 0, 0))],
        out_specs=blk(L),
        out_shape=jax.ShapeDtypeStruct((B, L, w), jnp.float32),
        scratch_shapes=[pltpu.VMEM((L, NA_ROWS * GRID_W + lc), jnp.float32),
                        pltpu.VMEM((L, NA_ROWS * GRID_W + lc), jnp.bfloat16)],
        compiler_params=pltpu.CompilerParams(dimension_semantics=("arbitrary", "arbitrary"),
                                             vmem_limit_bytes=VMEM_LIMIT_BYTES),
        name="neighbourhood_attention")(q, k, v, ctx_k, ctx_v, bias_tbl)


def _merge_kernel(ya_ref, yb_ref, yc_ref, gate_ref, x_ref, mod_ref, wb_ref, wo_ref, g_ref, b_ref, wqt_ref,
                  x1_ref, h2_ref, qt_ref):
    m = mod_ref[0]
    merged = None
    for i, y_ref in enumerate((ya_ref, yb_ref, yc_ref)):
        br = jnp.dot(y_ref[...].astype(jnp.bfloat16), wb_ref[i], preferred_element_type=jnp.float32)
        t = _sigmoid(gate_ref[:, i * D_MODEL:(i + 1) * D_MODEL]) * br
        merged = t if merged is None else merged + t
    y = jnp.dot(merged.astype(jnp.bfloat16), wo_ref[...], preferred_element_type=jnp.float32)
    x1 = _ln(DN_ALPHA * x_ref[...] + m[:, 2 * D_MODEL:3 * D_MODEL] * y) * g_ref[...] + b_ref[...]
    x1_ref[...] = x1
    h2 = (_ln(x1) * (1.0 + m[:, 4 * D_MODEL:5 * D_MODEL]) + m[:, 3 * D_MODEL:4 * D_MODEL]).astype(jnp.bfloat16)
    h2_ref[...] = h2
    qt_ref[...] = lax.dot_general(wqt_ref[...], h2, (((1,), (1,)), ((), ())), preferred_element_type=jnp.float32)


def merge_branches(ya, yb, yc, gate, x, mod, w_branch, w_out, ln_g, ln_b, wq_t, is_ctx):
    T, d = x.shape
    nq = wq_t.shape[0]
    row = lambda w: pl.BlockSpec((ROW_TILE, w), lambda i: (i, 0))
    full = lambda a: pl.BlockSpec(a.shape, lambda i: (0,) * a.ndim)
    return pl.pallas_call(
        _merge_kernel, grid=(T // ROW_TILE,),
        in_specs=[row(512), row(512), row(512), row(N_BRANCH * d), row(d),
                  pl.BlockSpec((1, 1, mod.shape[2]), lambda i: (_cond_row(i, is_ctx, ROW_TILE), 0, 0)),
                  full(w_branch), full(w_out), full(ln_g), full(ln_b), full(wq_t)],
        out_specs=[row(d), row(d), pl.BlockSpec((nq, ROW_TILE), lambda i: (0, i))],
        out_shape=[jax.ShapeDtypeStruct((T, d), jnp.float32), jax.ShapeDtypeStruct((T, d), jnp.bfloat16),
                   jax.ShapeDtypeStruct((nq, T), jnp.float32)],
        compiler_params=pltpu.CompilerParams(dimension_semantics=("arbitrary",), vmem_limit_bytes=VMEM_LIMIT_BYTES),
        name="merge_branches")(ya, yb, yc, gate, x, mod, w_branch, w_out, ln_g, ln_b, wq_t)


CAND_GROUPS = ((0, 1, 16), (1, 1, 8), (2, 1, 8), (3, 1, 8), (4, 1, 8), (5, 1, 8), (6, 1, 8), (7, 1, 8), (8, 8, 1))


def _cand_flat_index():
    rows = []
    for a0, na, nb in CAND_GROUPS:
        rows += [a * PEER_TOPK + b for a in range(a0, a0 + na) for b in range(nb)]
    needed = {(a, b) for a in range(PEER_TOPK) for b in range(PEER_TOPK) if (a + 1) * (b + 1) <= PEER_TOPK}
    assert needed <= {(r // PEER_TOPK, r % PEER_TOPK) for r in rows}
    return np.asarray(rows, np.float32)


def _topk_rows(s, ids, k):
    vals, idxs = [], []
    big = jnp.float32(1e9)
    for _ in range(k):
        m = jnp.max(s, axis=0, keepdims=True)
        idx = jnp.min(jnp.where(s == m, ids, big), axis=0, keepdims=True)
        vals.append(m)
        idxs.append(idx)
        s = jnp.where(ids == idx, -jnp.inf, s)
    return jnp.concatenate(vals, axis=0), jnp.concatenate(idxs, axis=0)


def _select_rows(table, sel):
    out = jnp.zeros(sel.shape, table.dtype)
    for a in range(table.shape[0]):
        out = jnp.where(sel == float(a), table[a:a + 1, :], out)
    return out


def _peer_topk_kernel(qt_ref, keys_ref, cid_ref, et_ref, gt_ref):
    key_ids = lax.broadcasted_iota(jnp.int32, (PEER_KEYS, TOPK_TOK), 0).astype(jnp.float32)

    def sub_key_topk(h):
        sv, si = [], []
        for p in range(2):
            q = qt_ref[pl.ds(pl.multiple_of(h * 256 + p * 128, 128), 128), :].astype(jnp.bfloat16)
            s = jnp.dot(keys_ref[h, p], q, preferred_element_type=jnp.float32)
            v, i = _topk_rows(s, key_ids, PEER_TOPK)
            sv.append(v)
            si.append(i)
        return sv, si

    def pair_topk(h, sv, si):
        groups = []
        for a0, na, nb in CAND_GROUPS:
            groups.append(sv[0][a0:a0 + na, :] + sv[1][0:nb, :])
        cand = jnp.concatenate(groups, axis=0)
        cv, ci = _topk_rows(cand, cid_ref[...], PEER_TOPK)
        a_sel = jnp.floor(ci * (1.0 / PEER_TOPK))
        b_sel = ci - a_sel * PEER_TOPK
        i_sel = _select_rows(si[0], a_sel)
        j_sel = _select_rows(si[1], b_sel)
        ex = jnp.exp(cv - cv[0:1, :])
        g = ex / jnp.sum(ex, axis=0, keepdims=True)
        row = pl.multiple_of(h * PEER_TOPK, PEER_TOPK)
        et_ref[pl.ds(row, PEER_TOPK), :] = (i_sel * PEER_KEYS + j_sel).astype(jnp.int32)
        gt_ref[pl.ds(row, PEER_TOPK), :] = g

    def head_pair(hp, carry):
        subs = [sub_key_topk(2 * hp + u) for u in range(2)]
        for u in range(2):
            pair_topk(2 * hp + u, *subs[u])
        return carry

    lax.fori_loop(0, PEER_HEADS // 2, head_pair, 0)


def peer_topk(qt, keys_bf16):
    T = qt.shape[1]
    cid = jnp.asarray(np.tile(_cand_flat_index()[:, None], (1, TOPK_TOK)))
    return pl.pallas_call(
        _peer_topk_kernel,
        grid=(T // TOPK_TOK,),
        in_specs=[pl.BlockSpec((qt.shape[0], TOPK_TOK), lambda i: (0, i)),
                  pl.BlockSpec(keys_bf16.shape, lambda i: (0, 0, 0, 0)),
                  pl.BlockSpec(cid.shape, lambda i: (0, 0))],
        out_specs=[pl.BlockSpec((PEER_HEADS * PEER_TOPK, TOPK_TOK), lambda i: (0, i)),
                   pl.BlockSpec((PEER_HEADS * PEER_TOPK, TOPK_TOK), lambda i: (0, i))],
        out_shape=[jax.ShapeDtypeStruct((PEER_HEADS * PEER_TOPK, T), jnp.int32),
                   jax.ShapeDtypeStruct((PEER_HEADS * PEER_TOPK, T), jnp.float32)],
        compiler_params=pltpu.CompilerParams(dimension_semantics=("arbitrary",)),
        name="peer_topk",
    )(qt, keys_bf16, cid)


def _peer_dense_kernel(x_ref, e_ref, g_ref, ut_ref, v_ref, x1_ref, mod_ref, lg_ref, lb_ref, o_ref, g3_ref, acc_ref,
                       *, t_blk, te):
    j = pl.program_id(1)
    nk = PEER_KEYS
    n_i = te // nk

    @pl.when(j == 0)
    def _build():
        acc_ref[...] = jnp.zeros_like(acc_ref)
        sub_iota = lax.broadcasted_iota(jnp.int32, (nk, nk), 0)
        zeros = jnp.zeros((nk, nk), jnp.bfloat16)

        def body(tp, carry):
            onehot_i, gated_j = [], []
            for u in range(2):
                e_row = e_ref[pl.ds(2 * tp + u, 1), :]
                g_row = g_ref[pl.ds(2 * tp + u, 1), :]
                i_row = lax.shift_right_logical(e_row, 7)
                j_row = lax.bitwise_and(e_row, nk - 1)
                onehot_i.append(jnp.where(sub_iota == i_row, 1.0, 0.0).astype(jnp.bfloat16))
                gated_j.append(jnp.where(sub_iota == j_row, g_row, 0.0).astype(jnp.bfloat16))
            lhs = jnp.concatenate(onehot_i, axis=1)
            rhs = jnp.concatenate([jnp.concatenate([gated_j[0], zeros], axis=1),
                                   jnp.concatenate([zeros, gated_j[1]], axis=1)], axis=0)
            gt = lax.dot_general(lhs, rhs, (((1,), (1,)), ((), ())), preferred_element_type=jnp.float32)
            for u in range(2):
                row0 = pl.multiple_of((2 * tp + u) * G3_PITCH, 8)
                g3_ref[pl.ds(row0, nk), :] = gt[:, u * nk:(u + 1) * nk]
            return carry

        lax.fori_loop(0, t_blk // 2, body, 0, unroll=16)

    s = jnp.dot(x_ref[...], ut_ref[...], preferred_element_type=jnp.float32)
    parts = []
    for k in range(n_i):
        gi = g3_ref[pl.ds(j * n_i + k, t_blk, stride=G3_PITCH), :]
        parts.append((_gelu_tanh(s[:, k * nk:(k + 1) * nk]) * gi).astype(jnp.bfloat16))
    a = jnp.concatenate(parts, axis=1)
    acc_ref[...] += jnp.dot(a, v_ref[...], preferred_element_type=jnp.float32)

    @pl.when(j == pl.num_programs(1) - 1)
    def _fin():
        gate2 = mod_ref[0][:, 5 * D_MODEL:6 * D_MODEL]
        o_ref[...] = _ln(DN_ALPHA * x1_ref[...] + gate2 * acc_ref[...]) * lg_ref[...] + lb_ref[...]


def peer_dense(x_bf16, e, g, ut_bf16, v_bf16, x1, mod, ln_g, ln_b, is_ctx):
    T, D = x_bf16.shape
    NE = v_bf16.shape[0]
    t_blk, te = PEER_TOK_BLOCK, PEER_EXPERT_TILE
    kern = functools.partial(_peer_dense_kernel, t_blk=t_blk, te=te)
    tok = lambda w: pl.BlockSpec((t_blk, w), lambda i, j: (i, 0))
    return pl.pallas_call(
        kern,
        grid=(T // t_blk, NE // te),
        in_specs=[tok(D), tok(LANES), tok(LANES),
                  pl.BlockSpec((D, te), lambda i, j: (0, j)),
                  pl.BlockSpec((te, D), lambda i, j: (j, 0)),
                  tok(D),
                  pl.BlockSpec((1, 1, mod.shape[2]), lambda i, j: (_cond_row(i, is_ctx, t_blk), 0, 0)),
                  pl.BlockSpec((1, D), lambda i, j: (0, 0)),
                  pl.BlockSpec((1, D), lambda i, j: (0, 0))],
        out_specs=tok(D),
        out_shape=jax.ShapeDtypeStruct((T, D), jnp.float32),
        scratch_shapes=[pltpu.VMEM((t_blk * G3_PITCH, PEER_KEYS), jnp.float32),
                        pltpu.VMEM((t_blk, D), jnp.float32)],
        compiler_params=pltpu.CompilerParams(dimension_semantics=("arbitrary", "arbitrary"),
                                             vmem_limit_bytes=VMEM_LIMIT_BYTES),
        name="peer_dense",
    )(x_bf16, e, g, ut_bf16, v_bf16, x1, mod, ln_g, ln_b)


def _in_proj_weight(w_in_l):
    offs = np.cumsum((0,) + IN_SIZES)
    col = lambda i: w_in_l[:, offs[i]:offs[i + 1]]
    dt_pad = jnp.pad(col(4), ((0, 0), (0, DT_PAD - IN_SIZES[4])))
    parts = [col(0), col(1), col(2), col(3), dt_pad] + [col(i) for i in range(5, 12)]
    return jnp.concatenate(parts, axis=1).astype(jnp.bfloat16)


def kernel(x_prompt, x_sample, cache_na_k, cache_na_v, cache_gqa_k, cache_gqa_v, state_ssd, c, c_ctx,
           w_mod, b_mod, w_in, conv_w, conv_b, ssd_a_log, ssd_dt_bias, ssd_d, ssd_norm_g, na_rpb,
           gqa_q_norm, gqa_k_norm, w_branch, w_out, ln1_g, ln1_b, ln2_g, ln2_b,
           peer_wq, peer_keys, peer_u, peer_v):
    bf = jnp.bfloat16
    cond = jnp.zeros((COND_ROWS, D_MODEL), jnp.float32).at[0].set(c_ctx).at[1:1 + DEC_BATCH].set(c)
    mod_all = adaln_table(cond, w_mod, b_mod)
    bias_all = na_bias_table(na_rpb)
    rope_q = rope_tables(DEC_SEQ, GQA_HEADS)
    rope_k = rope_tables(DEC_SEQ, GQA_KV_HEADS)
    zero_state = jnp.zeros((BATCH, 2, SSD_HEADS, SSD_HEAD_DIM, SSD_STATE), jnp.float32)
    na_ck = cache_na_k.reshape(DEC_BATCH, DEPTH, PAST_LEN, NA_HEADS * HEAD_DIM)
    na_cv = cache_na_v.reshape(DEC_BATCH, DEPTH, PAST_LEN, NA_HEADS * HEAD_DIM)
    gqa_ck = cache_gqa_k.reshape(DEC_BATCH, DEPTH, PAST_LEN, GQA_KV_HEADS * HEAD_DIM)
    gqa_cv = cache_gqa_v.reshape(DEC_BATCH, DEPTH, PAST_LEN, GQA_KV_HEADS * HEAD_DIM)

    streams = {True: x_prompt.reshape(BATCH * SEQ, D_MODEL), False: x_sample.reshape(DEC_BATCH * DEC_SEQ, D_MODEL)}
    new_na_k, new_na_v, new_g_k, new_g_v, new_ssd = [], [], [], [], []
    for l in range(DEPTH):
        mod = mod_all[l].reshape(COND_ROWS, 1, 6 * D_MODEL)
        w_cat = _in_proj_weight(w_in[l])
        wb, wo, wq_t = w_branch[l].astype(bf), w_out[l].astype(bf), peer_wq[l].T.astype(bf)
        keys_b = peer_keys[l].astype(bf)
        ut, vb = peer_u[l].T.astype(bf), peer_v[l].astype(bf)
        bias_tbl = bias_all[l]
        ssd_args = (conv_w[l], conv_b[l], ssd_a_log[l], ssd_dt_bias[l], ssd_d[l], ssd_norm_g[l])
        for is_ctx in (True, False):
            x = streams[is_ctx]
            nb, sl = (BATCH, SEQ) if is_ctx else (DEC_BATCH, DEC_SEQ)
            z, xs, bc, dt, naq, nak, nav, gq, gk, gv, gate = in_proj(x, mod, w_cat, is_ctx)
            r3 = lambda a: a.reshape(nb, sl, a.shape[-1])
            if is_ctx:
                ya, st = ssd_mixer(r3(z), r3(xs), r3(bc), r3(dt), *ssd_args, zero_state)
                yb = dense_attention(r3(naq), r3(nak), r3(nav))
                yc, gk_n = dense_attention(r3(gq), r3(gk), r3(gv), q_gain=gqa_q_norm[l], k_gain=gqa_k_norm[l])
                new_na_k.append(nak.reshape(BATCH, SEQ, NA_HEADS, HEAD_DIM))
                new_na_v.append(nav.reshape(BATCH, SEQ, NA_HEADS, HEAD_DIM))
                new_g_k.append(gk_n.reshape(BATCH, SEQ, GQA_KV_HEADS, HEAD_DIM))
                new_g_v.append(gv.reshape(BATCH, SEQ, GQA_KV_HEADS, HEAD_DIM))
                new_ssd.append(st)
            else:
                ya, _ = ssd_mixer(r3(z), r3(xs), r3(bc), r3(dt), *ssd_args, state_ssd, layer=l)
                yb = neighbourhood_attention(r3(naq), r3(nak), r3(nav), na_ck, na_cv, bias_tbl, l)
                yc, _ = dense_attention(r3(gq), r3(gk), r3(gv), gqa_ck, gqa_cv,
                                        gqa_q_norm[l], gqa_k_norm[l], rope_q + rope_k, layer=l)
            r2 = lambda a: a.reshape(nb * sl, a.shape[-1])
            x1, h2, qt = merge_branches(r2(ya), r2(yb), r2(yc), gate, x, mod, wb, wo,
                                        ln1_g[l][None, :], ln1_b[l][None, :], wq_t, is_ctx)
            et, gt = peer_topk(qt, keys_b)
            streams[is_ctx] = peer_dense(h2, et.T, gt.T, ut, vb, x1, mod,
                                         ln2_g[l][None, :], ln2_b[l][None, :], is_ctx)

    return (streams[True].reshape(BATCH, SEQ, D_MODEL), streams[False].reshape(DEC_BATCH, DEC_SEQ, D_MODEL),
            jnp.stack(new_na_k, axis=1), jnp.stack(new_na_v, axis=1),
            jnp.stack(new_g_k, axis=1), jnp.stack(new_g_v, axis=1), jnp.stack(new_ssd, axis=1))
```

```python
import functools

import jax
import jax.numpy as jnp
import numpy as np
from jax import lax
from jax.experimental import pallas as pl
from jax.experimental.pallas import tpu as pltpu

D_MODEL = 1024
BATCH = 16
SEQ = 256
DEPTH = 4
DEC_BATCH = 4
DEC_SEQ = 1024
PAST_LEN = 256
GRID_W = 64
HEAD_DIM = 64
N_BRANCH = 3
SSD_HEADS = 8
SSD_HEAD_DIM = 64
SSD_INNER = SSD_HEADS * SSD_HEAD_DIM
SSD_GROUPS = 2
SSD_STATE = 64
SSD_BC = SSD_GROUPS * SSD_STATE
SSD_CHUNK = 128
CONV_W = 5
NA_HEADS = 8
NA_ROWS = 8
NA_COLS = 16
GQA_HEADS = 8
GQA_KV_HEADS = 2
ROPE_THETA = 10000.0
PEER_HEADS = 8
PEER_KEYS = 128
PEER_TOPK = 16
DN_ALPHA = (2 * DEPTH) ** 0.25
EPS = 1e-6
ATTN_SCALE = HEAD_DIM ** -0.5
assert ATTN_SCALE == 0.125
IN_SIZES = (SSD_INNER, SSD_INNER, SSD_BC, SSD_BC, 2 * SSD_HEADS,
            NA_HEADS * HEAD_DIM, NA_HEADS * HEAD_DIM, NA_HEADS * HEAD_DIM,
            GQA_HEADS * HEAD_DIM, GQA_KV_HEADS * HEAD_DIM, GQA_KV_HEADS * HEAD_DIM,
            N_BRANCH * D_MODEL)

LANES = 128
VMEM_LIMIT_BYTES = 56 * 1024 * 1024
NEG_BIG = -1e30
ROW_TILE = 256
COND_ROWS = 8
DT_PAD = LANES
IN_OUT_WIDTHS = (SSD_INNER, SSD_INNER, 2 * SSD_BC, DT_PAD, 512, 512, 512, 512, 128, 128, N_BRANCH * D_MODEL)
PEER_TOK_BLOCK = 512
PEER_EXPERT_TILE = 512
G3_PITCH = 136
TOPK_TOK = LANES
TOPK_HEADS_PER_BLOCK = 4
NA_SOFTMAX_ROWS = 128


def _cond_row(i, is_ctx, rows_per_step):
    if is_ctx:
        return 0
    return 1 + i // (DEC_SEQ // rows_per_step)


def _ln(x):
    mu = jnp.mean(x, axis=-1, keepdims=True)
    xc = x - mu
    var = jnp.mean(xc * xc, axis=-1, keepdims=True)
    return xc * lax.rsqrt(var + EPS)


def _silu(x):
    return x / (1.0 + jnp.exp(-x))


def _sigmoid(x):
    return 1.0 / (1.0 + jnp.exp(-x))


def _gelu_tanh(x):
    return 0.5 * x * (1.0 + jnp.tanh(0.7978845608028654 * (x + 0.044715 * (x * x * x))))


def _adaln_kernel(c_ref, w_ref, b_ref, o_ref):
    a = _silu(c_ref[...]).astype(jnp.bfloat16)
    o_ref[0] = jnp.dot(a, w_ref[0].astype(jnp.bfloat16), preferred_element_type=jnp.float32) + b_ref[0]


def adaln_table(cond, w_mod, b_mod, tn=1536):
    depth, d, n = w_mod.shape
    return pl.pallas_call(
        _adaln_kernel, grid=(depth, n // tn),
        in_specs=[pl.BlockSpec((COND_ROWS, d), lambda l, j: (0, 0)),
                  pl.BlockSpec((1, d, tn), lambda l, j: (l, 0, j)),
                  pl.BlockSpec((1, 1, tn), lambda l, j: (l, 0, j))],
        out_specs=pl.BlockSpec((1, COND_ROWS, tn), lambda l, j: (l, 0, j)),
        out_shape=jax.ShapeDtypeStruct((depth, COND_ROWS, n), jnp.float32),
        compiler_params=pltpu.CompilerParams(dimension_semantics=("arbitrary", "arbitrary"),
                                             vmem_limit_bytes=VMEM_LIMIT_BYTES),
        name="adaln_table")(cond, w_mod, b_mod.reshape(depth, 1, n))


def _in_proj_kernel(x_ref, mod_ref, w_ref, *o_refs):
    m = mod_ref[0]
    h = _ln(x_ref[...]) * (1.0 + m[:, D_MODEL:2 * D_MODEL]) + m[:, 0:D_MODEL]
    p = jnp.dot(h.astype(jnp.bfloat16), w_ref[...], preferred_element_type=jnp.float32)
    off = 0
    for o_ref, w in zip(o_refs, IN_OUT_WIDTHS):
        o_ref[...] = p[:, off:off + w]
        off += w


def in_proj(x, mod, w_cat, is_ctx):
    T, d = x.shape
    n = w_cat.shape[1]
    return pl.pallas_call(
        _in_proj_kernel, grid=(T // ROW_TILE,),
        in_specs=[pl.BlockSpec((ROW_TILE, d), lambda i: (i, 0)),
                  pl.BlockSpec((1, 1, mod.shape[2]), lambda i: (_cond_row(i, is_ctx, ROW_TILE), 0, 0)),
                  pl.BlockSpec((d, n), lambda i: (0, 0), pipeline_mode=pl.Buffered(1))],
        out_specs=[pl.BlockSpec((ROW_TILE, w), lambda i: (i, 0)) for w in IN_OUT_WIDTHS],
        out_shape=[jax.ShapeDtypeStruct((T, w), jnp.float32) for w in IN_OUT_WIDTHS],
        compiler_params=pltpu.CompilerParams(dimension_semantics=("arbitrary",), vmem_limit_bytes=VMEM_LIMIT_BYTES),
        name="in_proj")(x, mod, w_cat)


def _dwconv_silu(x, w_ref, b_ref):
    L = x.shape[0]
    row = lax.broadcasted_iota(jnp.int32, x.shape, 0)
    acc = x * w_ref[CONV_W // 2:CONV_W // 2 + 1, :] + b_ref[...]
    for w in range(CONV_W):
        s = w - CONV_W // 2
        if s == 0:
            continue
        shifted = pltpu.roll(x, (-s) % L, 0)
        ok = (row + s >= 0) & (row + s < L)
        acc = acc + jnp.where(ok, shifted, 0.0) * w_ref[w:w + 1, :]
    return _silu(acc)


def _cumsum_rows(a, reverse):
    q = a.shape[0]
    row = lax.broadcasted_iota(jnp.int32, a.shape, 0)
    s = 1
    while s < q:
        if reverse:
            a = a + jnp.where(row < q - s, pltpu.roll(a, q - s, 0), 0.0)
        else:
            a = a + jnp.where(row >= s, pltpu.roll(a, s, 0), 0.0)
        s *= 2
    return a


def _ssd_kernel(z_ref, xs_ref, bc_ref, dt_ref, cwx_ref, cbx_ref, cwbc_ref, cbbc_ref, a_ref, dtb_ref, dsk_ref,
                ng_ref, h0_ref, y_ref, st_ref, xc_ref, bcc_ref, dtp_ref, *, n_chunks):
    Q = SSD_CHUNK
    P = SSD_HEAD_DIM
    N = SSD_STATE
    rep = SSD_HEADS // SSD_GROUPS
    xc_ref[...] = _dwconv_silu(xs_ref[0], cwx_ref, cbx_ref)
    bcc_ref[...] = _dwconv_silu(bc_ref[0], cwbc_ref, cbbc_ref)
    pre = dt_ref[0] + dtb_ref[...]
    dtp_ref[...] = jnp.maximum(pre, 0.0) + jnp.log1p(jnp.exp(-jnp.abs(pre)))
    y_ref[0] = xc_ref[...] * dsk_ref[...]
    st_ref[0] = h0_ref[0]
    ii = lax.broadcasted_iota(jnp.int32, (Q, Q), 0)
    jj = lax.broadcasted_iota(jnp.int32, (Q, Q), 1)
    nt = (((1,), (1,)), ((), ()))

    def chunk_pair(c, carry):
        for d in range(2):
            cc = c if d == 0 else n_chunks - 1 - c
            r0 = pl.multiple_of(cc * Q, Q)
            x = xc_ref[pl.ds(r0, Q), :]
            bcv = bcc_ref[pl.ds(r0, Q), :]
            dt = dtp_ref[pl.ds(r0, Q), :]
            a = dt * a_ref[...]
            cs = _cumsum_rows(a, reverse=(d == 1))
            tot = cs[Q - 1:Q, :] if d == 0 else cs[0:1, :]
            dec = jnp.exp(tot - cs)
            inn = jnp.exp(cs)
            cs_t = cs.T
            w_t = (dt * dec).T
            etot = jnp.exp(tot)
            x_t = x.T
            keep = (jj <= ii) if d == 0 else (jj >= ii)
            ys = []
            for g in range(SSD_GROUPS):
                bg = bcv[:, g * N:(g + 1) * N].astype(jnp.bfloat16)
                cg = bcv[:, SSD_BC + g * N:SSD_BC + (g + 1) * N].astype(jnp.bfloat16)
                cb = lax.dot_general(cg, bg, nt, preferred_element_type=jnp.float32)
                for hg in range(rep):
                    h = g * rep + hg
                    ln = d * SSD_HEADS + h
                    seg = cs[:, ln:ln + 1] - cs_t[ln:ln + 1, :]
                    lmat = jnp.where(keep, jnp.exp(jnp.where(keep, seg, 0.0)), 0.0)
                    xh = x[:, h * P:(h + 1) * P]
                    xd = (xh * dt[:, ln:ln + 1]).astype(jnp.bfloat16)
                    y_diag = jnp.dot((cb * lmat).astype(jnp.bfloat16), xd, preferred_element_type=jnp.float32)
                    hprev = st_ref[0, d, h]
                    y_off = lax.dot_general(cg, hprev.astype(jnp.bfloat16), nt,
                                            preferred_element_type=jnp.float32) * inn[:, ln:ln + 1]
                    ys.append(y_diag + y_off)
                    xw = (x_t[h * P:(h + 1) * P, :] * w_t[ln:ln + 1, :]).astype(jnp.bfloat16)
                    s_new = jnp.dot(xw, bg, preferred_element_type=jnp.float32)
                    st_ref[0, d, h] = hprev * etot[:, ln:ln + 1] + s_new
            y_ref[0, pl.ds(r0, Q), :] += jnp.concatenate(ys, axis=1)
        return carry

    lax.fori_loop(0, n_chunks, chunk_pair, 0)
    y = y_ref[0] * _silu(z_ref[0])
    y = y * lax.rsqrt(jnp.mean(y * y, axis=-1, keepdims=True) + EPS)
    y_ref[0] = y * ng_ref[...]


def ssd_mixer(z, xs, bc, dt_raw, conv_w, conv_b, a_log, dt_bias, d_skip, norm_g, h0, layer=None):
    B, L, _ = z.shape
    n_chunks = L // SSD_CHUNK
    pad = lambda v: jnp.pad(v.reshape(1, -1), ((0, 0), (0, LANES - v.size)))
    a_row = pad(-jnp.exp(a_log))
    dtb_row = pad(dt_bias)
    dsk = jnp.repeat(d_skip, SSD_HEAD_DIM)[None, :]
    seq = lambda w: pl.BlockSpec((1, L, w), lambda b: (b, 0, 0))
    full = lambda a: pl.BlockSpec(a.shape, lambda b: (0,) * a.ndim)
    cwx, cwbc = conv_w[:, :SSD_INNER], conv_w[:, SSD_INNER:]
    cbx, cbbc = conv_b[None, :SSD_INNER], conv_b[None, SSD_INNER:]
    st_shape = (2, SSD_HEADS, SSD_HEAD_DIM, SSD_STATE)
    st_spec = pl.BlockSpec((1,) + st_shape, lambda b: (b, 0, 0, 0, 0))
    h0_spec = st_spec if layer is None else pl.BlockSpec((1, None) + st_shape, lambda b: (b, layer, 0, 0, 0, 0))
    consts = [cwx, cbx, cwbc, cbbc, a_row, dtb_row, dsk, norm_g[None, :]]
    return pl.pallas_call(
        functools.partial(_ssd_kernel, n_chunks=n_chunks),
        grid=(B,),
        in_specs=[seq(SSD_INNER), seq(SSD_INNER), seq(2 * SSD_BC), seq(DT_PAD)] + [full(a) for a in consts]
        + [h0_spec],
        out_specs=[seq(SSD_INNER), st_spec],
        out_shape=[jax.ShapeDtypeStruct((B, L, SSD_INNER), jnp.float32),
                   jax.ShapeDtypeStruct((B,) + st_shape, jnp.float32)],
        scratch_shapes=[pltpu.VMEM((L, SSD_INNER), jnp.float32), pltpu.VMEM((L, 2 * SSD_BC), jnp.float32),
                        pltpu.VMEM((L, DT_PAD), jnp.float32)],
        compiler_params=pltpu.CompilerParams(dimension_semantics=("arbitrary",), vmem_limit_bytes=VMEM_LIMIT_BYTES),
        name="ssd_mixer")(z, xs, bc, dt_raw, *consts, h0)


def _head_rms(x, gain_row, n_heads):
    parts = []
    for h in range(n_heads):
        xh = x[:, h * HEAD_DIM:(h + 1) * HEAD_DIM]
        parts.append(xh * lax.rsqrt(jnp.mean(xh * xh, axis=-1, keepdims=True) + EPS))
    return jnp.concatenate(parts, axis=1) * gain_row


def _rope(x, cos, sin_signed, first16):
    n = x.shape[1]
    partner = jnp.where(first16, pltpu.roll(x, n - 16, 1), pltpu.roll(x, 16, 1))
    return x * cos + partner * sin_signed


def _softmax_rows(s):
    m = jnp.max(s, axis=-1, keepdims=True)
    e = jnp.exp(s - m)
    return e * (1.0 / jnp.sum(e, axis=-1, keepdims=True))


def _dense_attn_kernel(*refs, n_heads, n_kv, has_cache, norm, rope, q_blk):
    it = iter(refs)
    q_ref, k_ref, v_ref = next(it), next(it), next(it)
    ck_ref = cv_ref = qg_ref = kg_ref = cq_ref = sq_ref = ckk_ref = skk_ref = None
    if has_cache:
        ck_ref, cv_ref = next(it), next(it)
    if norm:
        qg_ref, kg_ref = next(it), next(it)
    if rope:
        cq_ref, sq_ref, ckk_ref, skk_ref = next(it), next(it), next(it), next(it)
    o_ref = next(it)
    kn_ref = next(it) if norm else None

    q = q_ref[0]
    k = k_ref[0]
    if norm:
        q = _head_rms(q, qg_ref[...], n_heads)
        k = _head_rms(k, kg_ref[...], n_kv)
        kn_ref[0] = k
    if rope:
        lane_q = lax.broadcasted_iota(jnp.int32, q.shape, 1)
        lane_k = lax.broadcasted_iota(jnp.int32, k.shape, 1)
        q = _rope(q, cq_ref[...], sq_ref[...], lax.bitwise_and(lane_q, 31) < 16)
        k = _rope(k, ckk_ref[...], skk_ref[...], lax.bitwise_and(lane_k, 31) < 16)
    qb = (q * ATTN_SCALE).astype(jnp.bfloat16)
    kb = k.astype(jnp.bfloat16)
    vb = v_ref[0].astype(jnp.bfloat16)
    if has_cache:
        kb = jnp.concatenate([ck_ref[0].astype(jnp.bfloat16), kb], axis=0)
        vb = jnp.concatenate([cv_ref[0].astype(jnp.bfloat16), vb], axis=0)
    lq = q.shape[0]
    grp = n_heads // n_kv
    outs = []
    for h in range(n_heads):
        kv = h // grp
        kh = kb[:, kv * HEAD_DIM:(kv + 1) * HEAD_DIM]
        vh = vb[:, kv * HEAD_DIM:(kv + 1) * HEAD_DIM]
        rows = []
        for r0 in range(0, lq, q_blk):
            qh = qb[r0:r0 + q_blk, h * HEAD_DIM:(h + 1) * HEAD_DIM]
            s = lax.dot_general(qh, kh, (((1,), (1,)), ((), ())), preferred_element_type=jnp.float32)
            p = _softmax_rows(s).astype(jnp.bfloat16)
            rows.append(jnp.dot(p, vh, preferred_element_type=jnp.float32))
        outs.append(rows[0] if len(rows) == 1 else jnp.concatenate(rows, axis=0))
    o_ref[0] = jnp.concatenate(outs, axis=1)


def dense_attention(q, k, v, cache_k=None, cache_v=None, q_gain=None, k_gain=None, rope_tabs=None, layer=0):
    B, L, qw = q.shape
    kw = k.shape[2]
    n_heads, n_kv = qw // HEAD_DIM, kw // HEAD_DIM
    has_cache, norm, rope = cache_k is not None, q_gain is not None, rope_tabs is not None
    args = [q, k, v]
    specs = [pl.BlockSpec((1, L, qw), lambda b: (b, 0, 0)),
             pl.BlockSpec((1, L, kw), lambda b: (b, 0, 0)),
             pl.BlockSpec((1, L, kw), lambda b: (b, 0, 0))]
    if has_cache:
        lc = cache_k.shape[2]
        args += [cache_k, cache_v]
        specs += [pl.BlockSpec((1, None, lc, kw), lambda b: (b, layer, 0, 0))] * 2
    if norm:
        args += [jnp.tile(q_gain, n_heads)[None, :], jnp.tile(k_gain, n_kv)[None, :]]
        specs += [pl.BlockSpec((1, qw), lambda b: (0, 0)), pl.BlockSpec((1, kw), lambda b: (0, 0))]
    if rope:
        args += list(rope_tabs)
        specs += [pl.BlockSpec((L, qw), lambda b: (0, 0))] * 2 + [pl.BlockSpec((L, kw), lambda b: (0, 0))] * 2
    out_shape = [jax.ShapeDtypeStruct((B, L, qw), jnp.float32)]
    out_specs = [pl.BlockSpec((1, L, qw), lambda b: (b, 0, 0))]
    if norm:
        out_shape.append(jax.ShapeDtypeStruct((B, L, kw), jnp.float32))
        out_specs.append(pl.BlockSpec((1, L, kw), lambda b: (b, 0, 0)))
    kern = functools.partial(_dense_attn_kernel, n_heads=n_heads, n_kv=n_kv, has_cache=has_cache, norm=norm,
                             rope=rope, q_blk=min(L, 256))
    res = pl.pallas_call(kern, grid=(B,), in_specs=specs, out_specs=out_specs, out_shape=out_shape,
                         compiler_params=pltpu.CompilerParams(dimension_semantics=("arbitrary",),
                                                              vmem_limit_bytes=VMEM_LIMIT_BYTES),
                         name="dense_attention")(*args)
    return res if norm else res[0]


def rope_tables(L, n_heads):
    t = np.arange(L)
    pos = np.stack([t // GRID_W, t % GRID_W], axis=1).astype(np.float32)
    half = HEAD_DIM // 2
    inv = 1.0 / (ROPE_THETA ** (jnp.arange(0, half, 2, dtype=jnp.float32) / half))
    ang = jnp.asarray(pos)[:, :, None] * inv[None, None, :]
    cos, sin = jnp.cos(ang), jnp.sin(ang)
    cos64 = jnp.concatenate([cos, cos], axis=2).reshape(L, HEAD_DIM)
    sin64 = jnp.concatenate([-sin, sin], axis=2).reshape(L, HEAD_DIM)
    return jnp.tile(cos64, (1, n_heads)), jnp.tile(sin64, (1, n_heads))


def na_bias_table(rpb):
    qcol = np.arange(GRID_W)[:, None]
    kc = np.arange(GRID_W)[None, :]
    qstart = np.clip(qcol - NA_COLS // 2, 0, GRID_W - NA_COLS)
    valid = (kc >= qstart) & (kc < qstart + NA_COLS)
    dc = np.clip(kc - qcol + NA_COLS - 1, 0, 2 * NA_COLS - 2)
    t = jnp.where(jnp.asarray(valid), rpb[..., dc], NEG_BIG)
    t = jnp.stack([t[..., off:off + NA_ROWS, :, :] for off in range(NA_ROWS)], axis=-4)
    t = jnp.swapaxes(t, -3, -2)
    return t.reshape(t.shape[:-2] + (NA_ROWS * GRID_W,))


def _na_kernel(q_ref, k_ref, v_ref, ck_ref, cv_ref, bias_ref, o_ref, s_ref, p_ref, *, rows):
    qb = (q_ref[0] * ATTN_SCALE).astype(jnp.bfloat16)
    kb = k_ref[0].astype(jnp.bfloat16)
    vb = v_ref[0].astype(jnp.bfloat16)
    ckb = ck_ref[0].astype(jnp.bfloat16)
    cvb = cv_ref[0].astype(jnp.bfloat16)
    n_loc = NA_ROWS * GRID_W
    L = qb.shape[0]
    nt = (((1,), (1,)), ((), ()))
    key_row0 = [min(max(r - NA_ROWS // 2, 0), rows - NA_ROWS) for r in range(rows)]
    outs = []
    for hh in range(2):
        sl = slice(hh * HEAD_DIM, (hh + 1) * HEAD_DIM)
        qh = qb[:, sl]
        s_ref[:, n_loc:] = lax.dot_general(qh, ckb[:, sl], nt, preferred_element_type=jnp.float32)
        for r, kr0 in enumerate(key_row0):
            kl = kb[kr0 * GRID_W:kr0 * GRID_W + n_loc, sl]
            s_loc = lax.dot_general(qh[r * GRID_W:(r + 1) * GRID_W], kl, nt, preferred_element_type=jnp.float32)
            s_ref[r * GRID_W:(r + 1) * GRID_W, :n_loc] = s_loc + bias_ref[hh, kr0 - r + NA_ROWS - 1]
        for r0 in range(0, L, NA_SOFTMAX_ROWS):
            p_ref[r0:r0 + NA_SOFTMAX_ROWS, :] = _softmax_rows(s_ref[r0:r0 + NA_SOFTMAX_ROWS, :]).astype(jnp.bfloat16)
        o_ctx = jnp.dot(p_ref[:, n_loc:], cvb[:, sl], preferred_element_type=jnp.float32)
        o_loc = []
        for r, kr0 in enumerate(key_row0):
            vl = vb[kr0 * GRID_W:kr0 * GRID_W + n_loc, sl]
            o_loc.append(jnp.dot(p_ref[r * GRID_W:(r + 1) * GRID_W, :n_loc], vl, preferred_element_type=jnp.float32))
        outs.append(o_ctx + jnp.concatenate(o_loc, axis=0))
    o_ref[0] = jnp.concatenate(outs, axis=1)


def neighbourhood_attention(q, k, v, ctx_k, ctx_v, bias_tbl, layer):
    B, L, w = q.shape
    lc = ctx_k.shape[2]
    n_pairs = w // LANES
    rows = L // GRID_W
    assert rows >= NA_ROWS
    kern = functools.partial(_na_kernel, rows=rows)
    blk = lambda n: pl.BlockSpec((1, n, LANES), lambda hp, b: (b, 0, hp))
    cache_blk = pl.BlockSpec((1, None, lc, LANES), lambda hp, b: (b, layer, 0, hp))
    return pl.pallas_call(
        kern, grid=(n_pairs, B),
        in_specs=[blk(L), blk(L), blk(L), cache_blk, cache_blk,
                  pl.BlockSpec((2, NA_ROWS, GRID_W, NA_ROWS * GRID_W), lambda hp, b: (hp, 0, 0, 0))],
        out_specs=blk(L),
        out_shape=jax.ShapeDtypeStruct((B, L, w), jnp.float32),
        scratch_shapes=[pltpu.VMEM((L, NA_ROWS * GRID_W + lc), jnp.float32),
                        pltpu.VMEM((L, NA_ROWS * GRID_W + lc), jnp.bfloat16)],
        compiler_params=pltpu.CompilerParams(dimension_semantics=("arbitrary", "arbitrary"),
                                             vmem_limit_bytes=VMEM_LIMIT_BYTES),
        name="neighbourhood_attention")(q, k, v, ctx_k, ctx_v, bias_tbl)


def _merge_kernel(ya_ref, yb_ref, yc_ref, gate_ref, x_ref, mod_ref, wb_ref, wo_ref, g_ref, b_ref, wqt_ref,
                  x1_ref, h2_ref, qt_ref):
    m = mod_ref[0]
    merged = None
    for i, y_ref in enumerate((ya_ref, yb_ref, yc_ref)):
        br = jnp.dot(y_ref[...].astype(jnp.bfloat16), wb_ref[i], preferred_element_type=jnp.float32)
        t = _sigmoid(gate_ref[:, i * D_MODEL:(i + 1) * D_MODEL]) * br
        merged = t if merged is None else merged + t
    y = jnp.dot(merged.astype(jnp.bfloat16), wo_ref[...], preferred_element_type=jnp.float32)
    x1 = _ln(DN_ALPHA * x_ref[...] + m[:, 2 * D_MODEL:3 * D_MODEL] * y) * g_ref[...] + b_ref[...]
    x1_ref[...] = x1
    h2 = (_ln(x1) * (1.0 + m[:, 4 * D_MODEL:5 * D_MODEL]) + m[:, 3 * D_MODEL:4 * D_MODEL]).astype(jnp.bfloat16)
    h2_ref[...] = h2
    qt_ref[...] = lax.dot_general(wqt_ref[...], h2, (((1,), (1,)), ((), ())), preferred_element_type=jnp.float32)


def merge_branches(ya, yb, yc, gate, x, mod, w_branch, w_out, ln_g, ln_b, wq_t, is_ctx):
    T, d = x.shape
    nq = wq_t.shape[0]
    row = lambda w: pl.BlockSpec((ROW_TILE, w), lambda i: (i, 0))
    full = lambda a: pl.BlockSpec(a.shape, lambda i: (0,) * a.ndim)
    return pl.pallas_call(
        _merge_kernel, grid=(T // ROW_TILE,),
        in_specs=[row(512), row(512), row(512), row(N_BRANCH * d), row(d),
                  pl.BlockSpec((1, 1, mod.shape[2]), lambda i: (_cond_row(i, is_ctx, ROW_TILE), 0, 0)),
                  full(w_branch), full(w_out), full(ln_g), full(ln_b), full(wq_t)],
        out_specs=[row(d), row(d), pl.BlockSpec((nq, ROW_TILE), lambda i: (0, i))],
        out_shape=[jax.ShapeDtypeStruct((T, d), jnp.float32), jax.ShapeDtypeStruct((T, d), jnp.bfloat16),
                   jax.ShapeDtypeStruct((nq, T), jnp.float32)],
        compiler_params=pltpu.CompilerParams(dimension_semantics=("arbitrary",), vmem_limit_bytes=VMEM_LIMIT_BYTES),
        name="merge_branches")(ya, yb, yc, gate, x, mod, w_branch, w_out, ln_g, ln_b, wq_t)


CAND_GROUPS = ((0, 1, 16), (1, 1, 8), (2, 1, 8), (3, 1, 8), (4, 1, 8), (5, 1, 8), (6, 1, 8), (7, 1, 8), (8, 8, 1))


def _cand_flat_index():
    rows = []
    for a0, na, nb in CAND_GROUPS:
        rows += [a * PEER_TOPK + b for a in range(a0, a0 + na) for b in range(nb)]
    needed = {(a, b) for a in range(PEER_TOPK) for b in range(PEER_TOPK) if (a + 1) * (b + 1) <= PEER_TOPK}
    assert needed <= {(r // PEER_TOPK, r % PEER_TOPK) for r in rows}
    return np.asarray(rows, np.float32)


def _topk_rows(s, ids, k):
    vals, idxs = [], []
    big = jnp.float32(1e9)
    for _ in range(k):
        m = jnp.max(s, axis=0, keepdims=True)
        idx = jnp.min(jnp.where(s == m, ids, big), axis=0, keepdims=True)
        vals.append(m)
        idxs.append(idx)
        s = jnp.where(ids == idx, -jnp.inf, s)
    return jnp.concatenate(vals, axis=0), jnp.concatenate(idxs, axis=0)


def _select_rows(table, sel):
    out = jnp.zeros(sel.shape, table.dtype)
    for a in range(table.shape[0]):
        out = jnp.where(sel == float(a), table[a:a + 1, :], out)
    return out


def _peer_topk_kernel(qt_ref, keys_ref, cid_ref, et_ref, gt_ref):
    key_ids = lax.broadcasted_iota(jnp.int32, (PEER_KEYS, TOPK_TOK), 0).astype(jnp.float32)

    def sub_key_topk(h):
        sv, si = [], []
        for p in range(2):
            q = qt_ref[pl.ds(pl.multiple_of(h * 256 + p * 128, 128), 128), :].astype(jnp.bfloat16)
            s = jnp.dot(keys_ref[h, p], q, preferred_element_type=jnp.float32)
            v, i = _topk_rows(s, key_ids, PEER_TOPK)
            sv.append(v)
            si.append(i)
        return sv, si

    def pair_topk(h, sv, si):
        groups = []
        for a0, na, nb in CAND_GROUPS:
            groups.append(sv[0][a0:a0 + na, :] + sv[1][0:nb, :])
        cand = jnp.concatenate(groups, axis=0)
        cv, ci = _topk_rows(cand, cid_ref[...], PEER_TOPK)
        a_sel = jnp.floor(ci * (1.0 / PEER_TOPK))
        b_sel = ci - a_sel * PEER_TOPK
        i_sel = _select_rows(si[0], a_sel)
        j_sel = _select_rows(si[1], b_sel)
        ex = jnp.exp(cv - cv[0:1, :])
        g = ex / jnp.sum(ex, axis=0, keepdims=True)
        row = pl.multiple_of(h * PEER_TOPK, PEER_TOPK)
        et_ref[pl.ds(row, PEER_TOPK), :] = (i_sel * PEER_KEYS + j_sel).astype(jnp.int32)
        gt_ref[pl.ds(row, PEER_TOPK), :] = g

    def head_group(hg, carry):
        heads = [TOPK_HEADS_PER_BLOCK * hg + u for u in range(TOPK_HEADS_PER_BLOCK)]
        subs = [sub_key_topk(h) for h in heads]
        for h, (sv, si) in zip(heads, subs):
            pair_topk(h, sv, si)
        return carry

    lax.fori_loop(0, PEER_HEADS // TOPK_HEADS_PER_BLOCK, head_group, 0)


def peer_topk(qt, keys_bf16):
    T = qt.shape[1]
    cid = jnp.asarray(np.tile(_cand_flat_index()[:, None], (1, TOPK_TOK)))
    return pl.pallas_call(
        _peer_topk_kernel,
        grid=(T // TOPK_TOK,),
        in_specs=[pl.BlockSpec((qt.shape[0], TOPK_TOK), lambda i: (0, i)),
                  pl.BlockSpec(keys_bf16.shape, lambda i: (0, 0, 0, 0)),
                  pl.BlockSpec(cid.shape, lambda i: (0, 0))],
        out_specs=[pl.BlockSpec((PEER_HEADS * PEER_TOPK, TOPK_TOK), lambda i: (0, i)),
                   pl.BlockSpec((PEER_HEADS * PEER_TOPK, TOPK_TOK), lambda i: (0, i))],
        out_shape=[jax.ShapeDtypeStruct((PEER_HEADS * PEER_TOPK, T), jnp.int32),
                   jax.ShapeDtypeStruct((PEER_HEADS * PEER_TOPK, T), jnp.float32)],
        compiler_params=pltpu.CompilerParams(dimension_semantics=("arbitrary",)),
        name="peer_topk",
    )(qt, keys_bf16, cid)


def _peer_dense_kernel(x_ref, e_ref, g_ref, u_ref, v_ref, x1_ref, mod_ref, lg_ref, lb_ref, o_ref, g3_ref, acc_ref,
                       *, t_blk, te):
    j = pl.program_id(1)
    nk = PEER_KEYS
    n_i = te // nk

    @pl.when(j == 0)
    def _build():
        acc_ref[...] = jnp.zeros_like(acc_ref)
        sub_iota = lax.broadcasted_iota(jnp.int32, (nk, nk), 0)
        zeros = jnp.zeros((nk, nk), jnp.bfloat16)

        def body(tp, carry):
            onehot_i, gated_j = [], []
            for u in range(2):
                e_row = e_ref[pl.ds(2 * tp + u, 1), :]
                g_row = g_ref[pl.ds(2 * tp + u, 1), :]
                i_row = lax.shift_right_logical(e_row, 7)
                j_row = lax.bitwise_and(e_row, nk - 1)
                onehot_i.append(jnp.where(sub_iota == i_row, 1.0, 0.0).astype(jnp.bfloat16))
                gated_j.append(jnp.where(sub_iota == j_row, g_row, 0.0).astype(jnp.bfloat16))
            lhs = jnp.concatenate(onehot_i, axis=1)
            rhs = jnp.concatenate([jnp.concatenate([gated_j[0], zeros], axis=1),
                                   jnp.concatenate([zeros, gated_j[1]], axis=1)], axis=0)
            gt = lax.dot_general(lhs, rhs, (((1,), (1,)), ((), ())), preferred_element_type=jnp.float32)
            for u in range(2):
                row0 = pl.multiple_of((2 * tp + u) * G3_PITCH, 8)
                g3_ref[pl.ds(row0, nk), :] = gt[:, u * nk:(u + 1) * nk]
            return carry

        lax.fori_loop(0, t_blk // 2, body, 0, unroll=16)

    s = lax.dot_general(x_ref[...], u_ref[...], (((1,), (1,)), ((), ())),
                        preferred_element_type=jnp.float32)
    parts = []
    for k in range(n_i):
        gi = g3_ref[pl.ds(j * n_i + k, t_blk, stride=G3_PITCH), :]
        parts.append((_gelu_tanh(s[:, k * nk:(k + 1) * nk]) * gi).astype(jnp.bfloat16))
    a = jnp.concatenate(parts, axis=1)
    acc_ref[...] += jnp.dot(a, v_ref[...], preferred_element_type=jnp.float32)

    @pl.when(j == pl.num_programs(1) - 1)
    def _fin():
        gate2 = mod_ref[0][:, 5 * D_MODEL:6 * D_MODEL]
        o_ref[...] = _ln(DN_ALPHA * x1_ref[...] + gate2 * acc_ref[...]) * lg_ref[...] + lb_ref[...]


def peer_dense(x_bf16, e, g, u_bf16, v_bf16, x1, mod, ln_g, ln_b, is_ctx):
    T, D = x_bf16.shape
    NE = v_bf16.shape[0]
    t_blk, te = PEER_TOK_BLOCK, PEER_EXPERT_TILE
    kern = functools.partial(_peer_dense_kernel, t_blk=t_blk, te=te)
    tok = lambda w: pl.BlockSpec((t_blk, w), lambda i, j: (i, 0))
    return pl.pallas_call(
        kern,
        grid=(T // t_blk, NE // te),
        in_specs=[tok(D), tok(LANES), tok(LANES),
                  pl.BlockSpec((te, D), lambda i, j: (j, 0)),
                  pl.BlockSpec((te, D), lambda i, j: (j, 0)),
                  tok(D),
                  pl.BlockSpec((1, 1, mod.shape[2]), lambda i, j: (_cond_row(i, is_ctx, t_blk), 0, 0)),
                  pl.BlockSpec((1, D), lambda i, j: (0, 0)),
                  pl.BlockSpec((1, D), lambda i, j: (0, 0))],
        out_specs=tok(D),
        out_shape=jax.ShapeDtypeStruct((T, D), jnp.float32),
        scratch_shapes=[pltpu.VMEM((t_blk * G3_PITCH, PEER_KEYS), jnp.float32),
                        pltpu.VMEM((t_blk, D), jnp.float32)],
        compiler_params=pltpu.CompilerParams(dimension_semantics=("arbitrary", "arbitrary"),
                                             vmem_limit_bytes=VMEM_LIMIT_BYTES),
        name="peer_dense",
    )(x_bf16, e, g, u_bf16, v_bf16, x1, mod, ln_g, ln_b)


def _in_proj_weight(w_in_l):
    offs = np.cumsum((0,) + IN_SIZES)
    col = lambda i: w_in_l[:, offs[i]:offs[i + 1]]
    dt_pad = jnp.pad(col(4), ((0, 0), (0, DT_PAD - IN_SIZES[4])))
    parts = [col(0), col(1), col(2), col(3), dt_pad] + [col(i) for i in range(5, 12)]
    return jnp.concatenate(parts, axis=1).astype(jnp.bfloat16)


def kernel(x_prompt, x_sample, cache_na_k, cache_na_v, cache_gqa_k, cache_gqa_v, state_ssd, c, c_ctx,
           w_mod, b_mod, w_in, conv_w, conv_b, ssd_a_log, ssd_dt_bias, ssd_d, ssd_norm_g, na_rpb,
           gqa_q_norm, gqa_k_norm, w_branch, w_out, ln1_g, ln1_b, ln2_g, ln2_b,
           peer_wq, peer_keys, peer_u, peer_v):
    bf = jnp.bfloat16
    cond = jnp.zeros((COND_ROWS, D_MODEL), jnp.float32).at[0].set(c_ctx).at[1:1 + DEC_BATCH].set(c)
    mod_all = adaln_table(cond, w_mod, b_mod)
    bias_all = na_bias_table(na_rpb)
    rope_q = rope_tables(DEC_SEQ, GQA_HEADS)
    rope_k = rope_tables(DEC_SEQ, GQA_KV_HEADS)
    zero_state = jnp.zeros((BATCH, 2, SSD_HEADS, SSD_HEAD_DIM, SSD_STATE), jnp.float32)
    na_ck = cache_na_k.reshape(DEC_BATCH, DEPTH, PAST_LEN, NA_HEADS * HEAD_DIM)
    na_cv = cache_na_v.reshape(DEC_BATCH, DEPTH, PAST_LEN, NA_HEADS * HEAD_DIM)
    gqa_ck = cache_gqa_k.reshape(DEC_BATCH, DEPTH, PAST_LEN, GQA_KV_HEADS * HEAD_DIM)
    gqa_cv = cache_gqa_v.reshape(DEC_BATCH, DEPTH, PAST_LEN, GQA_KV_HEADS * HEAD_DIM)

    streams = {True: x_prompt.reshape(BATCH * SEQ, D_MODEL), False: x_sample.reshape(DEC_BATCH * DEC_SEQ, D_MODEL)}
    new_na_k, new_na_v, new_g_k, new_g_v, new_ssd = [], [], [], [], []
    for l in range(DEPTH):
        mod = mod_all[l].reshape(COND_ROWS, 1, 6 * D_MODEL)
        w_cat = _in_proj_weight(w_in[l])
        wb, wo, wq_t = w_branch[l].astype(bf), w_out[l].astype(bf), peer_wq[l].T.astype(bf)
        keys_b = peer_keys[l].astype(bf)
        ub, vb = peer_u[l].astype(bf), peer_v[l].astype(bf)
        bias_tbl = bias_all[l]
        ssd_args = (conv_w[l], conv_b[l], ssd_a_log[l], ssd_dt_bias[l], ssd_d[l], ssd_norm_g[l])
        for is_ctx in (True, False):
            x = streams[is_ctx]
            nb, sl = (BATCH, SEQ) if is_ctx else (DEC_BATCH, DEC_SEQ)
            z, xs, bc, dt, naq, nak, nav, gq, gk, gv, gate = in_proj(x, mod, w_cat, is_ctx)
            r3 = lambda a: a.reshape(nb, sl, a.shape[-1])
            if is_ctx:
                ya, st = ssd_mixer(r3(z), r3(xs), r3(bc), r3(dt), *ssd_args, zero_state)
                yb = dense_attention(r3(naq), r3(nak), r3(nav))
                yc, gk_n = dense_attention(r3(gq), r3(gk), r3(gv), q_gain=gqa_q_norm[l], k_gain=gqa_k_norm[l])
                new_na_k.append(nak.reshape(BATCH, SEQ, NA_HEADS, HEAD_DIM))
                new_na_v.append(nav.reshape(BATCH, SEQ, NA_HEADS, HEAD_DIM))
                new_g_k.append(gk_n.reshape(BATCH, SEQ, GQA_KV_HEADS, HEAD_DIM))
                new_g_v.append(gv.reshape(BATCH, SEQ, GQA_KV_HEADS, HEAD_DIM))
                new_ssd.append(st)
            else:
                ya, _ = ssd_mixer(r3(z), r3(xs), r3(bc), r3(dt), *ssd_args, state_ssd, layer=l)
                yb = neighbourhood_attention(r3(naq), r3(nak), r3(nav), na_ck, na_cv, bias_tbl, l)
                yc, _ = dense_attention(r3(gq), r3(gk), r3(gv), gqa_ck, gqa_cv,
                                        gqa_q_norm[l], gqa_k_norm[l], rope_q + rope_k, layer=l)
            r2 = lambda a: a.reshape(nb * sl, a.shape[-1])
            x1, h2, qt = merge_branches(r2(ya), r2(yb), r2(yc), gate, x, mod, wb, wo,
                                        ln1_g[l][None, :], ln1_b[l][None, :], wq_t, is_ctx)
            et, gt = peer_topk(qt, keys_b)
            streams[is_ctx] = peer_dense(h2, et.T, gt.T, ub, vb, x1, mod,
                                         ln2_g[l][None, :], ln2_b[l][None, :], is_ctx)

    return (streams[True].reshape(BATCH, SEQ, D_MODEL), streams[False].reshape(DEC_BATCH, DEC_SEQ, D_MODEL),
            jnp.stack(new_na_k, axis=1), jnp.stack(new_na_v, axis=1),
            jnp.stack(new_g_k, axis=1), jnp.stack(new_g_v, axis=1), jnp.stack(new_ssd, axis=1))
```

```python
import functools

import jax
import jax.numpy as jnp
import numpy as np
from jax import lax
from jax.experimental import pallas as pl
from jax.experimental.pallas import tpu as pltpu

D_MODEL = 1024
BATCH = 16
SEQ = 256
DEPTH = 4
DEC_BATCH = 4
DEC_SEQ = 1024
PAST_LEN = 256
GRID_W = 64
HEAD_DIM = 64
N_BRANCH = 3
SSD_HEADS = 8
SSD_HEAD_DIM = 64
SSD_INNER = SSD_HEADS * SSD_HEAD_DIM
SSD_GROUPS = 2
SSD_STATE = 64
SSD_BC = SSD_GROUPS * SSD_STATE
SSD_CHUNK = 128
CONV_W = 5
NA_HEADS = 8
NA_ROWS = 8
NA_COLS = 16
GQA_HEADS = 8
GQA_KV_HEADS = 2
ROPE_THETA = 10000.0
PEER_HEADS = 8
PEER_KEYS = 128
PEER_TOPK = 16
DN_ALPHA = (2 * DEPTH) ** 0.25
EPS = 1e-6
ATTN_SCALE = HEAD_DIM ** -0.5
assert ATTN_SCALE == 0.125
IN_SIZES = (SSD_INNER, SSD_INNER, SSD_BC, SSD_BC, 2 * SSD_HEADS,
            NA_HEADS * HEAD_DIM, NA_HEADS * HEAD_DIM, NA_HEADS * HEAD_DIM,
            GQA_HEADS * HEAD_DIM, GQA_KV_HEADS * HEAD_DIM, GQA_KV_HEADS * HEAD_DIM,
            N_BRANCH * D_MODEL)

LANES = 128
VMEM_LIMIT_BYTES = 56 * 1024 * 1024
NEG_BIG = -1e30
ROW_TILE = 256
COND_ROWS = 8
DT_PAD = LANES
IN_OUT_WIDTHS = (SSD_INNER, SSD_INNER, 2 * SSD_BC, DT_PAD, 512, 512, 512, 512, 128, 128, N_BRANCH * D_MODEL)
PEER_TOK_BLOCK = 512
PEER_EXPERT_TILE = 512
G3_PITCH = 136
TOPK_TOK = LANES
TOPK_HEADS_PER_BLOCK = 4
NA_SOFTMAX_ROWS = 128


def _cond_row(i, is_ctx, rows_per_step):
    if is_ctx:
        return 0
    return 1 + i // (DEC_SEQ // rows_per_step)


def _ln(x):
    mu = jnp.mean(x, axis=-1, keepdims=True)
    xc = x - mu
    var = jnp.mean(xc * xc, axis=-1, keepdims=True)
    return xc * lax.rsqrt(var + EPS)


def _silu(x):
    return x / (1.0 + jnp.exp(-x))


def _sigmoid(x):
    return 1.0 / (1.0 + jnp.exp(-x))


def _gelu_tanh(x):
    return 0.5 * x * (1.0 + jnp.tanh(0.7978845608028654 * (x + 0.044715 * (x * x * x))))


def _adaln_kernel(c_ref, w_ref, b_ref, o_ref):
    a = _silu(c_ref[...]).astype(jnp.bfloat16)
    o_ref[0] = jnp.dot(a, w_ref[0].astype(jnp.bfloat16), preferred_element_type=jnp.float32) + b_ref[0]


def adaln_table(cond, w_mod, b_mod, tn=1536):
    depth, d, n = w_mod.shape
    return pl.pallas_call(
        _adaln_kernel, grid=(depth, n // tn),
        in_specs=[pl.BlockSpec((COND_ROWS, d), lambda l, j: (0, 0)),
                  pl.BlockSpec((1, d, tn), lambda l, j: (l, 0, j)),
                  pl.BlockSpec((1, 1, tn), lambda l, j: (l, 0, j))],
        out_specs=pl.BlockSpec((1, COND_ROWS, tn), lambda l, j: (l, 0, j)),
        out_shape=jax.ShapeDtypeStruct((depth, COND_ROWS, n), jnp.float32),
        compiler_params=pltpu.CompilerParams(dimension_semantics=("arbitrary", "arbitrary"),
                                             vmem_limit_bytes=VMEM_LIMIT_BYTES),
        name="adaln_table")(cond, w_mod, b_mod.reshape(depth, 1, n))


def _in_proj_kernel(x_ref, mod_ref, w_ref, *o_refs):
    m = mod_ref[0]
    h = _ln(x_ref[...]) * (1.0 + m[:, D_MODEL:2 * D_MODEL]) + m[:, 0:D_MODEL]
    p = jnp.dot(h.astype(jnp.bfloat16), w_ref[...], preferred_element_type=jnp.float32)
    off = 0
    for o_ref, w in zip(o_refs, IN_OUT_WIDTHS):
        o_ref[...] = p[:, off:off + w]
        off += w


def _mod_spec(mod, layer, is_ctx, rows_per_step):
    return pl.BlockSpec((None, 1, 1, mod.shape[3]),
                        lambda i, *_: (layer, _cond_row(i, is_ctx, rows_per_step), 0, 0))


def _layer_spec(a, layer):
    return pl.BlockSpec((None,) + a.shape[1:], lambda *_: (layer,) + (0,) * (a.ndim - 1))


def in_proj(x, mod, w_cat, layer, is_ctx):
    T, d = x.shape
    n = w_cat.shape[2]
    return pl.pallas_call(
        _in_proj_kernel, grid=(T // ROW_TILE,),
        in_specs=[pl.BlockSpec((ROW_TILE, d), lambda i: (i, 0)),
                  _mod_spec(mod, layer, is_ctx, ROW_TILE),
                  pl.BlockSpec((None, d, n), lambda i: (layer, 0, 0), pipeline_mode=pl.Buffered(1))],
        out_specs=[pl.BlockSpec((ROW_TILE, w), lambda i: (i, 0)) for w in IN_OUT_WIDTHS],
        out_shape=[jax.ShapeDtypeStruct((T, w), jnp.float32) for w in IN_OUT_WIDTHS],
        compiler_params=pltpu.CompilerParams(dimension_semantics=("arbitrary",), vmem_limit_bytes=VMEM_LIMIT_BYTES),
        name="in_proj")(x, mod, w_cat)


def _dwconv_silu(x, w_ref, b_ref):
    L = x.shape[0]
    row = lax.broadcasted_iota(jnp.int32, x.shape, 0)
    acc = x * w_ref[CONV_W // 2:CONV_W // 2 + 1, :] + b_ref[...]
    for w in range(CONV_W):
        s = w - CONV_W // 2
        if s == 0:
            continue
        shifted = pltpu.roll(x, (-s) % L, 0)
        ok = (row + s >= 0) & (row + s < L)
        acc = acc + jnp.where(ok, shifted, 0.0) * w_ref[w:w + 1, :]
    return _silu(acc)


def _cumsum_rows(a, reverse):
    q = a.shape[0]
    row = lax.broadcasted_iota(jnp.int32, a.shape, 0)
    s = 1
    while s < q:
        if reverse:
            a = a + jnp.where(row < q - s, pltpu.roll(a, q - s, 0), 0.0)
        else:
            a = a + jnp.where(row >= s, pltpu.roll(a, s, 0), 0.0)
        s *= 2
    return a


def _ssd_kernel(z_ref, xs_ref, bc_ref, dt_ref, cwx_ref, cbx_ref, cwbc_ref, cbbc_ref, a_ref, dtb_ref, dsk_ref,
                ng_ref, h0_ref, y_ref, st_ref, xc_ref, bcc_ref, dtp_ref, *, n_chunks):
    Q = SSD_CHUNK
    P = SSD_HEAD_DIM
    N = SSD_STATE
    rep = SSD_HEADS // SSD_GROUPS
    xc_ref[...] = _dwconv_silu(xs_ref[0], cwx_ref, cbx_ref)
    bcc_ref[...] = _dwconv_silu(bc_ref[0], cwbc_ref, cbbc_ref)
    pre = dt_ref[0] + dtb_ref[...]
    dtp_ref[...] = jnp.maximum(pre, 0.0) + jnp.log1p(jnp.exp(-jnp.abs(pre)))
    y_ref[0] = xc_ref[...] * dsk_ref[...]
    st_ref[0] = h0_ref[0]
    ii = lax.broadcasted_iota(jnp.int32, (Q, Q), 0)
    jj = lax.broadcasted_iota(jnp.int32, (Q, Q), 1)
    nt = (((1,), (1,)), ((), ()))

    def chunk_pair(c, carry):
        for d in range(2):
            cc = c if d == 0 else n_chunks - 1 - c
            r0 = pl.multiple_of(cc * Q, Q)
            x = xc_ref[pl.ds(r0, Q), :]
            bcv = bcc_ref[pl.ds(r0, Q), :]
            dt = dtp_ref[pl.ds(r0, Q), :]
            a = dt * a_ref[...]
            cs = _cumsum_rows(a, reverse=(d == 1))
            tot = cs[Q - 1:Q, :] if d == 0 else cs[0:1, :]
            dec = jnp.exp(tot - cs)
            inn = jnp.exp(cs)
            cs_t = cs.T
            w_t = (dt * dec).T
            etot = jnp.exp(tot)
            x_t = x.T
            keep = (jj <= ii) if d == 0 else (jj >= ii)
            ys = []
            for g in range(SSD_GROUPS):
                bg = bcv[:, g * N:(g + 1) * N].astype(jnp.bfloat16)
                cg = bcv[:, SSD_BC + g * N:SSD_BC + (g + 1) * N].astype(jnp.bfloat16)
                cb = lax.dot_general(cg, bg, nt, preferred_element_type=jnp.float32)
                for hg in range(rep):
                    h = g * rep + hg
                    ln = d * SSD_HEADS + h
                    seg = cs[:, ln:ln + 1] - cs_t[ln:ln + 1, :]
                    lmat = jnp.where(keep, jnp.exp(jnp.where(keep, seg, 0.0)), 0.0)
                    xh = x[:, h * P:(h + 1) * P]
                    xd = (xh * dt[:, ln:ln + 1]).astype(jnp.bfloat16)
                    y_diag = jnp.dot((cb * lmat).astype(jnp.bfloat16), xd, preferred_element_type=jnp.float32)
                    hprev = st_ref[0, d, h]
                    y_off = lax.dot_general(cg, hprev.astype(jnp.bfloat16), nt,
                                            preferred_element_type=jnp.float32) * inn[:, ln:ln + 1]
                    ys.append(y_diag + y_off)
                    xw = (x_t[h * P:(h + 1) * P, :] * w_t[ln:ln + 1, :]).astype(jnp.bfloat16)
                    s_new = jnp.dot(xw, bg, preferred_element_type=jnp.float32)
                    st_ref[0, d, h] = hprev * etot[:, ln:ln + 1] + s_new
            y_ref[0, pl.ds(r0, Q), :] += jnp.concatenate(ys, axis=1)
        return carry

    lax.fori_loop(0, n_chunks, chunk_pair, 0)
    y = y_ref[0] * _silu(z_ref[0])
    y = y * lax.rsqrt(jnp.mean(y * y, axis=-1, keepdims=True) + EPS)
    y_ref[0] = y * ng_ref[...]


def ssd_mixer(z, xs, bc, dt_raw, conv_w, conv_b, a_log, dt_bias, d_skip, norm_g, h0, layer=None):
    B, L, _ = z.shape
    n_chunks = L // SSD_CHUNK
    pad = lambda v: jnp.pad(v.reshape(1, -1), ((0, 0), (0, LANES - v.size)))
    a_row = pad(-jnp.exp(a_log))
    dtb_row = pad(dt_bias)
    dsk = jnp.repeat(d_skip, SSD_HEAD_DIM)[None, :]
    seq = lambda w: pl.BlockSpec((1, L, w), lambda b: (b, 0, 0))
    full = lambda a: pl.BlockSpec(a.shape, lambda b: (0,) * a.ndim)
    cwx, cwbc = conv_w[:, :SSD_INNER], conv_w[:, SSD_INNER:]
    cbx, cbbc = conv_b[None, :SSD_INNER], conv_b[None, SSD_INNER:]
    st_shape = (2, SSD_HEADS, SSD_HEAD_DIM, SSD_STATE)
    st_spec = pl.BlockSpec((1,) + st_shape, lambda b: (b, 0, 0, 0, 0))
    h0_spec = st_spec if layer is None else pl.BlockSpec((1, None) + st_shape, lambda b: (b, layer, 0, 0, 0, 0))
    consts = [cwx, cbx, cwbc, cbbc, a_row, dtb_row, dsk, norm_g[None, :]]
    return pl.pallas_call(
        functools.partial(_ssd_kernel, n_chunks=n_chunks),
        grid=(B,),
        in_specs=[seq(SSD_INNER), seq(SSD_INNER), seq(2 * SSD_BC), seq(DT_PAD)] + [full(a) for a in consts]
        + [h0_spec],
        out_specs=[seq(SSD_INNER), st_spec],
        out_shape=[jax.ShapeDtypeStruct((B, L, SSD_INNER), jnp.float32),
                   jax.ShapeDtypeStruct((B,) + st_shape, jnp.float32)],
        scratch_shapes=[pltpu.VMEM((L, SSD_INNER), jnp.float32), pltpu.VMEM((L, 2 * SSD_BC), jnp.float32),
                        pltpu.VMEM((L, DT_PAD), jnp.float32)],
        compiler_params=pltpu.CompilerParams(dimension_semantics=("arbitrary",), vmem_limit_bytes=VMEM_LIMIT_BYTES),
        name="ssd_mixer")(z, xs, bc, dt_raw, *consts, h0)


def _head_rms(x, gain_row, n_heads):
    parts = []
    for h in range(n_heads):
        xh = x[:, h * HEAD_DIM:(h + 1) * HEAD_DIM]
        parts.append(xh * lax.rsqrt(jnp.mean(xh * xh, axis=-1, keepdims=True) + EPS))
    return jnp.concatenate(parts, axis=1) * gain_row


def _rope(x, cos, sin_signed, first16):
    n = x.shape[1]
    partner = jnp.where(first16, pltpu.roll(x, n - 16, 1), pltpu.roll(x, 16, 1))
    return x * cos + partner * sin_signed


def _softmax_rows(s):
    m = jnp.max(s, axis=-1, keepdims=True)
    e = jnp.exp(s - m)
    return e * (1.0 / jnp.sum(e, axis=-1, keepdims=True))


def _dense_attn_kernel(*refs, n_heads, n_kv, has_cache, norm, rope, q_blk):
    it = iter(refs)
    q_ref, k_ref, v_ref = next(it), next(it), next(it)
    ck_ref = cv_ref = qg_ref = kg_ref = cq_ref = sq_ref = ckk_ref = skk_ref = None
    if has_cache:
        ck_ref, cv_ref = next(it), next(it)
    if norm:
        qg_ref, kg_ref = next(it), next(it)
    if rope:
        cq_ref, sq_ref, ckk_ref, skk_ref = next(it), next(it), next(it), next(it)
    o_ref = next(it)
    kn_ref = next(it) if norm else None

    q = q_ref[0]
    k = k_ref[0]
    if norm:
        q = _head_rms(q, qg_ref[...], n_heads)
        k = _head_rms(k, kg_ref[...], n_kv)
        kn_ref[0] = k
    if rope:
        lane_q = lax.broadcasted_iota(jnp.int32, q.shape, 1)
        lane_k = lax.broadcasted_iota(jnp.int32, k.shape, 1)
        q = _rope(q, cq_ref[...], sq_ref[...], lax.bitwise_and(lane_q, 31) < 16)
        k = _rope(k, ckk_ref[...], skk_ref[...], lax.bitwise_and(lane_k, 31) < 16)
    qb = (q * ATTN_SCALE).astype(jnp.bfloat16)
    kb = k.astype(jnp.bfloat16)
    vb = v_ref[0].astype(jnp.bfloat16)
    if has_cache:
        kb = jnp.concatenate([ck_ref[0].astype(jnp.bfloat16), kb], axis=0)
        vb = jnp.concatenate([cv_ref[0].astype(jnp.bfloat16), vb], axis=0)
    lq = q.shape[0]
    grp = n_heads // n_kv
    outs = []
    for h in range(n_heads):
        kv = h // grp
        kh = kb[:, kv * HEAD_DIM:(kv + 1) * HEAD_DIM]
        vh = vb[:, kv * HEAD_DIM:(kv + 1) * HEAD_DIM]
        rows = []
        for r0 in range(0, lq, q_blk):
            qh = qb[r0:r0 + q_blk, h * HEAD_DIM:(h + 1) * HEAD_DIM]
            s = lax.dot_general(qh, kh, (((1,), (1,)), ((), ())), preferred_element_type=jnp.float32)
            p = _softmax_rows(s).astype(jnp.bfloat16)
            rows.append(jnp.dot(p, vh, preferred_element_type=jnp.float32))
        outs.append(rows[0] if len(rows) == 1 else jnp.concatenate(rows, axis=0))
    o_ref[0] = jnp.concatenate(outs, axis=1)


def dense_attention(q, k, v, cache_k=None, cache_v=None, q_gain=None, k_gain=None, rope_tabs=None, layer=0):
    B, L, qw = q.shape
    kw = k.shape[2]
    n_heads, n_kv = qw // HEAD_DIM, kw // HEAD_DIM
    has_cache, norm, rope = cache_k is not None, q_gain is not None, rope_tabs is not None
    args = [q, k, v]
    specs = [pl.BlockSpec((1, L, qw), lambda b: (b, 0, 0)),
             pl.BlockSpec((1, L, kw), lambda b: (b, 0, 0)),
             pl.BlockSpec((1, L, kw), lambda b: (b, 0, 0))]
    if has_cache:
        lc = cache_k.shape[2]
        args += [cache_k, cache_v]
        specs += [pl.BlockSpec((1, None, lc, kw), lambda b: (b, layer, 0, 0))] * 2
    if norm:
        args += [jnp.tile(q_gain, n_heads)[None, :], jnp.tile(k_gain, n_kv)[None, :]]
        specs += [pl.BlockSpec((1, qw), lambda b: (0, 0)), pl.BlockSpec((1, kw), lambda b: (0, 0))]
    if rope:
        args += list(rope_tabs)
        specs += [pl.BlockSpec((L, qw), lambda b: (0, 0))] * 2 + [pl.BlockSpec((L, kw), lambda b: (0, 0))] * 2
    out_shape = [jax.ShapeDtypeStruct((B, L, qw), jnp.float32)]
    out_specs = [pl.BlockSpec((1, L, qw), lambda b: (b, 0, 0))]
    if norm:
        out_shape.append(jax.ShapeDtypeStruct((B, L, kw), jnp.float32))
        out_specs.append(pl.BlockSpec((1, L, kw), lambda b: (b, 0, 0)))
    kern = functools.partial(_dense_attn_kernel, n_heads=n_heads, n_kv=n_kv, has_cache=has_cache, norm=norm,
                             rope=rope, q_blk=min(L, 256))
    res = pl.pallas_call(kern, grid=(B,), in_specs=specs, out_specs=out_specs, out_shape=out_shape,
                         compiler_params=pltpu.CompilerParams(dimension_semantics=("arbitrary",),
                                                              vmem_limit_bytes=VMEM_LIMIT_BYTES),
                         name="dense_attention")(*args)
    return res if norm else res[0]


def rope_tables(L, n_heads):
    t = np.arange(L)
    pos = np.stack([t // GRID_W, t % GRID_W], axis=1).astype(np.float32)
    half = HEAD_DIM // 2
    inv = 1.0 / (ROPE_THETA ** (jnp.arange(0, half, 2, dtype=jnp.float32) / half))
    ang = jnp.asarray(pos)[:, :, None] * inv[None, None, :]
    cos, sin = jnp.cos(ang), jnp.sin(ang)
    cos64 = jnp.concatenate([cos, cos], axis=2).reshape(L, HEAD_DIM)
    sin64 = jnp.concatenate([-sin, sin], axis=2).reshape(L, HEAD_DIM)
    return jnp.tile(cos64, (1, n_heads)), jnp.tile(sin64, (1, n_heads))


def na_bias_table(rpb):
    qcol = np.arange(GRID_W)[:, None]
    kc = np.arange(GRID_W)[None, :]
    qstart = np.clip(qcol - NA_COLS // 2, 0, GRID_W - NA_COLS)
    valid = (kc >= qstart) & (kc < qstart + NA_COLS)
    dc = np.clip(kc - qcol + NA_COLS - 1, 0, 2 * NA_COLS - 2)
    t = jnp.where(jnp.asarray(valid), rpb[..., dc], NEG_BIG)
    return jnp.concatenate([t[..., :-1, :, :], t[..., 1:, :, :]], axis=-1)


def _na_kernel(q_ref, k_ref, v_ref, ck_ref, cv_ref, bias_ref, o_ref, s_ref, p_ref, *, rows):
    qb = (q_ref[0] * ATTN_SCALE).astype(jnp.bfloat16)
    kb = k_ref[0].astype(jnp.bfloat16)
    vb = v_ref[0].astype(jnp.bfloat16)
    ckb = ck_ref[0].astype(jnp.bfloat16)
    cvb = cv_ref[0].astype(jnp.bfloat16)
    n_loc = NA_ROWS * GRID_W
    L = qb.shape[0]
    nt = (((1,), (1,)), ((), ()))
    key_row0 = [min(max(r - NA_ROWS // 2, 0), rows - NA_ROWS) for r in range(rows)]
    outs = []
    for hh in range(2):
        sl = slice(hh * HEAD_DIM, (hh + 1) * HEAD_DIM)
        qh = qb[:, sl]
        s_ref[:, n_loc:] = lax.dot_general(qh, ckb[:, sl], nt, preferred_element_type=jnp.float32)
        for r, kr0 in enumerate(key_row0):
            kl = kb[kr0 * GRID_W:kr0 * GRID_W + n_loc, sl]
            s_loc = lax.dot_general(qh[r * GRID_W:(r + 1) * GRID_W], kl, nt, preferred_element_type=jnp.float32)
            dr0 = kr0 - r + NA_ROWS - 1
            for c in range(NA_ROWS // 2):
                s_ref[r * GRID_W:(r + 1) * GRID_W, c * LANES:(c + 1) * LANES] = (
                    s_loc[:, c * LANES:(c + 1) * LANES] + bias_ref[hh, dr0 + 2 * c])
        for r0 in range(0, L, NA_SOFTMAX_ROWS):
            p_ref[r0:r0 + NA_SOFTMAX_ROWS, :] = _softmax_rows(s_ref[r0:r0 + NA_SOFTMAX_ROWS, :]).astype(jnp.bfloat16)
        o_ctx = jnp.dot(p_ref[:, n_loc:], cvb[:, sl], preferred_element_type=jnp.float32)
        o_loc = []
        for r, kr0 in enumerate(key_row0):
            vl = vb[kr0 * GRID_W:kr0 * GRID_W + n_loc, sl]
            o_loc.append(jnp.dot(p_ref[r * GRID_W:(r + 1) * GRID_W, :n_loc], vl, preferred_element_type=jnp.float32))
        outs.append(o_ctx + jnp.concatenate(o_loc, axis=0))
    o_ref[0] = jnp.concatenate(outs, axis=1)


def neighbourhood_attention(q, k, v, ctx_k, ctx_v, bias_tbl, layer):
    B, L, w = q.shape
    lc = ctx_k.shape[2]
    n_pairs = w // LANES
    rows = L // GRID_W
    assert rows >= NA_ROWS
    kern = functools.partial(_na_kernel, rows=rows)
    blk = lambda n: pl.BlockSpec((1, n, LANES), lambda hp, b: (b, 0, hp))
    cache_blk = pl.BlockSpec((1, None, lc, LANES), lambda hp, b: (b, layer, 0, hp))
    return pl.pallas_call(
        kern, grid=(n_pairs, B),
        in_specs=[blk(L), blk(L), blk(L), cache_blk, cache_blk,
                  pl.BlockSpec((None, 2) + bias_tbl.shape[2:], lambda hp, b: (layer, hp, 0, 0, 0))],
        out_specs=blk(L),
        out_shape=jax.ShapeDtypeStruct((B, L, w), jnp.float32),
        scratch_shapes=[pltpu.VMEM((L, NA_ROWS * GRID_W + lc), jnp.float32),
                        pltpu.VMEM((L, NA_ROWS * GRID_W + lc), jnp.bfloat16)],
        compiler_params=pltpu.CompilerParams(dimension_semantics=("arbitrary", "arbitrary"),
                                             vmem_limit_bytes=VMEM_LIMIT_BYTES),
        name="neighbourhood_attention")(q, k, v, ctx_k, ctx_v, bias_tbl)


def _merge_kernel(ya_ref, yb_ref, yc_ref, gate_ref, x_ref, mod_ref, wb_ref, wo_ref, g_ref, b_ref, wqt_ref,
                  x1_ref, h2_ref, qt_ref):
    m = mod_ref[0]
    merged = None
    for i, y_ref in enumerate((ya_ref, yb_ref, yc_ref)):
        br = jnp.dot(y_ref[...].astype(jnp.bfloat16), wb_ref[i], preferred_element_type=jnp.float32)
        t = _sigmoid(gate_ref[:, i * D_MODEL:(i + 1) * D_MODEL]) * br
        merged = t if merged is None else merged + t
    y = jnp.dot(merged.astype(jnp.bfloat16), wo_ref[...], preferred_element_type=jnp.float32)
    x1 = _ln(DN_ALPHA * x_ref[...] + m[:, 2 * D_MODEL:3 * D_MODEL] * y) * g_ref[...] + b_ref[...]
    x1_ref[...] = x1
    h2 = (_ln(x1) * (1.0 + m[:, 4 * D_MODEL:5 * D_MODEL]) + m[:, 3 * D_MODEL:4 * D_MODEL]).astype(jnp.bfloat16)
    h2_ref[...] = h2
    qt_ref[...] = lax.dot_general(wqt_ref[...], h2, (((1,), (1,)), ((), ())), preferred_element_type=jnp.float32)


def merge_branches(ya, yb, yc, gate, x, mod, w_branch, w_out, ln_g, ln_b, wq_t, layer, is_ctx):
    T, d = x.shape
    nq = wq_t.shape[1]
    row = lambda w: pl.BlockSpec((ROW_TILE, w), lambda i: (i, 0))
    full = lambda a: _layer_spec(a, layer)
    return pl.pallas_call(
        _merge_kernel, grid=(T // ROW_TILE,),
        in_specs=[row(512), row(512), row(512), row(N_BRANCH * d), row(d),
                  _mod_spec(mod, layer, is_ctx, ROW_TILE),
                  full(w_branch), full(w_out), full(ln_g), full(ln_b), full(wq_t)],
        out_specs=[row(d), row(d), pl.BlockSpec((nq, ROW_TILE), lambda i: (0, i))],
        out_shape=[jax.ShapeDtypeStruct((T, d), jnp.float32), jax.ShapeDtypeStruct((T, d), jnp.bfloat16),
                   jax.ShapeDtypeStruct((nq, T), jnp.float32)],
        compiler_params=pltpu.CompilerParams(dimension_semantics=("arbitrary",), vmem_limit_bytes=VMEM_LIMIT_BYTES),
        name="merge_branches")(ya, yb, yc, gate, x, mod, w_branch, w_out, ln_g, ln_b, wq_t)


CAND_GROUPS = ((0, 1, 16), (1, 1, 8), (2, 1, 8), (3, 1, 8), (4, 1, 8), (5, 1, 8), (6, 1, 8), (7, 1, 8), (8, 8, 1))


def _cand_flat_index():
    rows = []
    for a0, na, nb in CAND_GROUPS:
        rows += [a * PEER_TOPK + b for a in range(a0, a0 + na) for b in range(nb)]
    needed = {(a, b) for a in range(PEER_TOPK) for b in range(PEER_TOPK) if (a + 1) * (b + 1) <= PEER_TOPK}
    assert needed <= {(r // PEER_TOPK, r % PEER_TOPK) for r in rows}
    return np.asarray(rows, np.float32)


def _topk_rows(s, ids, k):
    vals, idxs = [], []
    big = jnp.float32(1e9)
    for _ in range(k):
        m = jnp.max(s, axis=0, keepdims=True)
        idx = jnp.min(jnp.where(s == m, ids, big), axis=0, keepdims=True)
        vals.append(m)
        idxs.append(idx)
        s = jnp.where(ids == idx, -jnp.inf, s)
    return jnp.concatenate(vals, axis=0), jnp.concatenate(idxs, axis=0)


def _select_rows(table, sel):
    out = jnp.zeros(sel.shape, table.dtype)
    for a in range(table.shape[0]):
        out = jnp.where(sel == float(a), table[a:a + 1, :], out)
    return out


def _peer_topk_kernel(qt_ref, keys_ref, cid_ref, et_ref, gt_ref):
    key_ids = lax.broadcasted_iota(jnp.int32, (PEER_KEYS, TOPK_TOK), 0).astype(jnp.float32)

    def sub_key_topk(h):
        sv, si = [], []
        for p in range(2):
            q = qt_ref[pl.ds(pl.multiple_of(h * 256 + p * 128, 128), 128), :].astype(jnp.bfloat16)
            s = jnp.dot(keys_ref[h, p], q, preferred_element_type=jnp.float32)
            v, i = _topk_rows(s, key_ids, PEER_TOPK)
            sv.append(v)
            si.append(i)
        return sv, si

    def pair_topk(h, sv, si):
        groups = []
        for a0, na, nb in CAND_GROUPS:
            groups.append(sv[0][a0:a0 + na, :] + sv[1][0:nb, :])
        cand = jnp.concatenate(groups, axis=0)
        cv, ci = _topk_rows(cand, cid_ref[...], PEER_TOPK)
        a_sel = jnp.floor(ci * (1.0 / PEER_TOPK))
        b_sel = ci - a_sel * PEER_TOPK
        i_sel = _select_rows(si[0], a_sel)
        j_sel = _select_rows(si[1], b_sel)
        ex = jnp.exp(cv - cv[0:1, :])
        g = ex / jnp.sum(ex, axis=0, keepdims=True)
        row = pl.multiple_of(h * PEER_TOPK, PEER_TOPK)
        et_ref[pl.ds(row, PEER_TOPK), :] = (i_sel * PEER_KEYS + j_sel).astype(jnp.int32)
        gt_ref[pl.ds(row, PEER_TOPK), :] = g

    def head_group(hg, carry):
        heads = [TOPK_HEADS_PER_BLOCK * hg + u for u in range(TOPK_HEADS_PER_BLOCK)]
        subs = [sub_key_topk(h) for h in heads]
        for h, (sv, si) in zip(heads, subs):
            pair_topk(h, sv, si)
        return carry

    lax.fori_loop(0, PEER_HEADS // TOPK_HEADS_PER_BLOCK, head_group, 0)


def peer_topk(qt, keys_bf16, layer):
    T = qt.shape[1]
    cid = jnp.asarray(np.tile(_cand_flat_index()[:, None], (1, TOPK_TOK)))
    return pl.pallas_call(
        _peer_topk_kernel,
        grid=(T // TOPK_TOK,),
        in_specs=[pl.BlockSpec((qt.shape[0], TOPK_TOK), lambda i: (0, i)),
                  _layer_spec(keys_bf16, layer),
                  pl.BlockSpec(cid.shape, lambda i: (0, 0))],
        out_specs=[pl.BlockSpec((PEER_HEADS * PEER_TOPK, TOPK_TOK), lambda i: (0, i)),
                   pl.BlockSpec((PEER_HEADS * PEER_TOPK, TOPK_TOK), lambda i: (0, i))],
        out_shape=[jax.ShapeDtypeStruct((PEER_HEADS * PEER_TOPK, T), jnp.int32),
                   jax.ShapeDtypeStruct((PEER_HEADS * PEER_TOPK, T), jnp.float32)],
        compiler_params=pltpu.CompilerParams(dimension_semantics=("arbitrary",)),
        name="peer_topk",
    )(qt, keys_bf16, cid)


def _peer_dense_kernel(x_ref, e_ref, g_ref, ut_ref, v_ref, x1_ref, mod_ref, lg_ref, lb_ref, o_ref, g3_ref, acc_ref,
                       *, t_blk, te):
    j = pl.program_id(1)
    nk = PEER_KEYS
    n_i = te // nk

    @pl.when(j == 0)
    def _build():
        acc_ref[...] = jnp.zeros_like(acc_ref)
        sub_iota = lax.broadcasted_iota(jnp.int32, (nk, nk), 0)
        zeros = jnp.zeros((nk, nk), jnp.bfloat16)

        def body(tp, carry):
            onehot_i, gated_j = [], []
            for u in range(2):
                e_row = e_ref[pl.ds(2 * tp + u, 1), :]
                g_row = g_ref[pl.ds(2 * tp + u, 1), :]
                i_row = lax.shift_right_logical(e_row, 7)
                j_row = lax.bitwise_and(e_row, nk - 1)
                onehot_i.append(jnp.where(sub_iota == i_row, 1.0, 0.0).astype(jnp.bfloat16))
                gated_j.append(jnp.where(sub_iota == j_row, g_row, 0.0).astype(jnp.bfloat16))
            lhs = jnp.concatenate(onehot_i, axis=1)
            rhs = jnp.concatenate([jnp.concatenate([gated_j[0], zeros], axis=1),
                                   jnp.concatenate([zeros, gated_j[1]], axis=1)], axis=0)
            gt = lax.dot_general(lhs, rhs, (((1,), (1,)), ((), ())), preferred_element_type=jnp.float32)
            for u in range(2):
                row0 = pl.multiple_of((2 * tp + u) * G3_PITCH, 8)
                g3_ref[pl.ds(row0, nk), :] = gt[:, u * nk:(u + 1) * nk]
            return carry

        lax.fori_loop(0, t_blk // 2, body, 0, unroll=16)

    s = jnp.dot(x_ref[...], ut_ref[...], preferred_element_type=jnp.float32)
    parts = []
    for k in range(n_i):
        gi = g3_ref[pl.ds(j * n_i + k, t_blk, stride=G3_PITCH), :]
        parts.append((_gelu_tanh(s[:, k * nk:(k + 1) * nk]) * gi).astype(jnp.bfloat16))
    a = jnp.concatenate(parts, axis=1)
    acc_ref[...] += jnp.dot(a, v_ref[...], preferred_element_type=jnp.float32)

    @pl.when(j == pl.num_programs(1) - 1)
    def _fin():
        gate2 = mod_ref[0][:, 5 * D_MODEL:6 * D_MODEL]
        o_ref[...] = _ln(DN_ALPHA * x1_ref[...] + gate2 * acc_ref[...]) * lg_ref[...] + lb_ref[...]


def peer_dense(x_bf16, e, g, ut_bf16, v_bf16, x1, mod, ln_g, ln_b, layer, is_ctx):
    T, D = x_bf16.shape
    NE = v_bf16.shape[1]
    t_blk, te = PEER_TOK_BLOCK, PEER_EXPERT_TILE
    kern = functools.partial(_peer_dense_kernel, t_blk=t_blk, te=te)
    tok = lambda w: pl.BlockSpec((t_blk, w), lambda i, j: (i, 0))
    return pl.pallas_call(
        kern,
        grid=(T // t_blk, NE // te),
        in_specs=[tok(D), tok(LANES), tok(LANES),
                  pl.BlockSpec((None, D, te), lambda i, j: (layer, 0, j)),
                  pl.BlockSpec((None, te, D), lambda i, j: (layer, j, 0)),
                  tok(D),
                  _mod_spec(mod, layer, is_ctx, t_blk),
                  _layer_spec(ln_g, layer),
                  _layer_spec(ln_b, layer)],
        out_specs=tok(D),
        out_shape=jax.ShapeDtypeStruct((T, D), jnp.float32),
        scratch_shapes=[pltpu.VMEM((t_blk * G3_PITCH, PEER_KEYS), jnp.float32),
                        pltpu.VMEM((t_blk, D), jnp.float32)],
        compiler_params=pltpu.CompilerParams(dimension_semantics=("arbitrary", "arbitrary"),
                                             vmem_limit_bytes=VMEM_LIMIT_BYTES),
        name="peer_dense",
    )(x_bf16, e, g, ut_bf16, v_bf16, x1, mod, ln_g, ln_b)


def _in_proj_weight(w_in):
    dt0 = sum(IN_SIZES[:4])
    dt1 = dt0 + IN_SIZES[4]
    pad = jnp.zeros(w_in.shape[:2] + (DT_PAD - IN_SIZES[4],), w_in.dtype)
    return jnp.concatenate([w_in[..., :dt1], pad, w_in[..., dt1:]], axis=-1).astype(jnp.bfloat16)


def kernel(x_prompt, x_sample, cache_na_k, cache_na_v, cache_gqa_k, cache_gqa_v, state_ssd, c, c_ctx,
           w_mod, b_mod, w_in, conv_w, conv_b, ssd_a_log, ssd_dt_bias, ssd_d, ssd_norm_g, na_rpb,
           gqa_q_norm, gqa_k_norm, w_branch, w_out, ln1_g, ln1_b, ln2_g, ln2_b,
           peer_wq, peer_keys, peer_u, peer_v):
    bf = jnp.bfloat16
    cond = jnp.zeros((COND_ROWS, D_MODEL), jnp.float32).at[0].set(c_ctx).at[1:1 + DEC_BATCH].set(c)
    mod = adaln_table(cond, w_mod, b_mod).reshape(DEPTH, COND_ROWS, 1, 6 * D_MODEL)
    bias_tbl = na_bias_table(na_rpb)
    w_cat = _in_proj_weight(w_in)
    wb, wo, wq_t = w_branch.astype(bf), w_out.astype(bf), jnp.swapaxes(peer_wq, 1, 2).astype(bf)
    keys_b = peer_keys.astype(bf)
    ut, vb = jnp.swapaxes(peer_u, 1, 2).astype(bf), peer_v.astype(bf)
    ln1 = (ln1_g[:, None, :], ln1_b[:, None, :])
    ln2 = (ln2_g[:, None, :], ln2_b[:, None, :])
    rope_q = rope_tables(DEC_SEQ, GQA_HEADS)
    rope_k = rope_tables(DEC_SEQ, GQA_KV_HEADS)
    zero_state = jnp.zeros((BATCH, 2, SSD_HEADS, SSD_HEAD_DIM, SSD_STATE), jnp.float32)
    na_ck = cache_na_k.reshape(DEC_BATCH, DEPTH, PAST_LEN, NA_HEADS * HEAD_DIM)
    na_cv = cache_na_v.reshape(DEC_BATCH, DEPTH, PAST_LEN, NA_HEADS * HEAD_DIM)
    gqa_ck = cache_gqa_k.reshape(DEC_BATCH, DEPTH, PAST_LEN, GQA_KV_HEADS * HEAD_DIM)
    gqa_cv = cache_gqa_v.reshape(DEC_BATCH, DEPTH, PAST_LEN, GQA_KV_HEADS * HEAD_DIM)

    streams = {True: x_prompt.reshape(BATCH * SEQ, D_MODEL), False: x_sample.reshape(DEC_BATCH * DEC_SEQ, D_MODEL)}
    new_na_k, new_na_v, new_g_k, new_g_v, new_ssd = [], [], [], [], []
    for l in range(DEPTH):
        ssd_args = (conv_w[l], conv_b[l], ssd_a_log[l], ssd_dt_bias[l], ssd_d[l], ssd_norm_g[l])
        for is_ctx in (True, False):
            x = streams[is_ctx]
            nb, sl = (BATCH, SEQ) if is_ctx else (DEC_BATCH, DEC_SEQ)
            z, xs, bc, dt, naq, nak, nav, gq, gk, gv, gate = in_proj(x, mod, w_cat, l, is_ctx)
            r3 = lambda a: a.reshape(nb, sl, a.shape[-1])
            if is_ctx:
                ya, st = ssd_mixer(r3(z), r3(xs), r3(bc), r3(dt), *ssd_args, zero_state)
                yb = dense_attention(r3(naq), r3(nak), r3(nav))
                yc, gk_n = dense_attention(r3(gq), r3(gk), r3(gv), q_gain=gqa_q_norm[l], k_gain=gqa_k_norm[l])
                new_na_k.append(nak.reshape(BATCH, SEQ, NA_HEADS, HEAD_DIM))
                new_na_v.append(nav.reshape(BATCH, SEQ, NA_HEADS, HEAD_DIM))
                new_g_k.append(gk_n.reshape(BATCH, SEQ, GQA_KV_HEADS, HEAD_DIM))
                new_g_v.append(gv.reshape(BATCH, SEQ, GQA_KV_HEADS, HEAD_DIM))
                new_ssd.append(st)
            else:
                ya, _ = ssd_mixer(r3(z), r3(xs), r3(bc), r3(dt), *ssd_args, state_ssd, layer=l)
                yb = neighbourhood_attention(r3(naq), r3(nak), r3(nav), na_ck, na_cv, bias_tbl, l)
                yc, _ = dense_attention(r3(gq), r3(gk), r3(gv), gqa_ck, gqa_cv,
                                        gqa_q_norm[l], gqa_k_norm[l], rope_q + rope_k, layer=l)
            r2 = lambda a: a.reshape(nb * sl, a.shape[-1])
            x1, h2, qt = merge_branches(r2(ya), r2(yb), r2(yc), gate, x, mod, wb, wo, *ln1, wq_t, l, is_ctx)
            et, gt = peer_topk(qt, keys_b, l)
            streams[is_ctx] = peer_dense(h2, et.T, gt.T, ut, vb, x1, mod, *ln2, l, is_ctx)

    return (streams[True].reshape(BATCH, SEQ, D_MODEL), streams[False].reshape(DEC_BATCH, DEC_SEQ, D_MODEL),
            jnp.stack(new_na_k, axis=1), jnp.stack(new_na_v, axis=1),
            jnp.stack(new_g_k, axis=1), jnp.stack(new_g_v, axis=1), jnp.stack(new_ssd, axis=1))
```

```python
import functools

import jax
import jax.numpy as jnp
import numpy as np
from jax import lax
from jax.experimental import pallas as pl
from jax.experimental.pallas import tpu as pltpu

D_MODEL = 1024
BATCH = 16
SEQ = 256
DEPTH = 4
DEC_BATCH = 4
DEC_SEQ = 1024
PAST_LEN = 256
GRID_W = 64
HEAD_DIM = 64
N_BRANCH = 3
SSD_HEADS = 8
SSD_HEAD_DIM = 64
SSD_INNER = SSD_HEADS * SSD_HEAD_DIM
SSD_GROUPS = 2
SSD_STATE = 64
SSD_BC = SSD_GROUPS * SSD_STATE
SSD_CHUNK = 128
CONV_W = 5
NA_HEADS = 8
NA_ROWS = 8
NA_COLS = 16
GQA_HEADS = 8
GQA_KV_HEADS = 2
ROPE_THETA = 10000.0
PEER_HEADS = 8
PEER_KEYS = 128
PEER_TOPK = 16
DN_ALPHA = (2 * DEPTH) ** 0.25
EPS = 1e-6
ATTN_SCALE = HEAD_DIM ** -0.5
assert ATTN_SCALE == 0.125
IN_SIZES = (SSD_INNER, SSD_INNER, SSD_BC, SSD_BC, 2 * SSD_HEADS,
            NA_HEADS * HEAD_DIM, NA_HEADS * HEAD_DIM, NA_HEADS * HEAD_DIM,
            GQA_HEADS * HEAD_DIM, GQA_KV_HEADS * HEAD_DIM, GQA_KV_HEADS * HEAD_DIM,
            N_BRANCH * D_MODEL)

LANES = 128
VMEM_LIMIT_BYTES = 56 * 1024 * 1024
NEG_BIG = -1e30
ROW_TILE = 256
COND_ROWS = 8
DT_PAD = LANES
IN_OUT_WIDTHS = (SSD_INNER, SSD_INNER, 2 * SSD_BC, DT_PAD, 512, 512, 512, 512, 128, 128, N_BRANCH * D_MODEL)
PEER_TOK_BLOCK = 512
PEER_EXPERT_TILE = 512
G3_PITCH = 136
TOPK_TOK = LANES
PEER_SUB_BLOCKS = PEER_TOK_BLOCK // TOPK_TOK
TOPK_HEADS_PER_BLOCK = 4
NA_SOFTMAX_ROWS = 128


def _cond_row(i, is_ctx, rows_per_step):
    if is_ctx:
        return 0
    return 1 + i // (DEC_SEQ // rows_per_step)


def _ln(x):
    mu = jnp.mean(x, axis=-1, keepdims=True)
    xc = x - mu
    var = jnp.mean(xc * xc, axis=-1, keepdims=True)
    return xc * lax.rsqrt(var + EPS)


def _silu(x):
    return x / (1.0 + jnp.exp(-x))


def _sigmoid(x):
    return 1.0 / (1.0 + jnp.exp(-x))


def _gelu_tanh(x):
    return 0.5 * x * (1.0 + jnp.tanh(0.7978845608028654 * (x + 0.044715 * (x * x * x))))


def _adaln_kernel(c_ref, w_ref, b_ref, o_ref):
    a = _silu(c_ref[...]).astype(jnp.bfloat16)
    o_ref[0] = jnp.dot(a, w_ref[0].astype(jnp.bfloat16), preferred_element_type=jnp.float32) + b_ref[0]


def adaln_table(cond, w_mod, b_mod, tn=1536):
    depth, d, n = w_mod.shape
    return pl.pallas_call(
        _adaln_kernel, grid=(depth, n // tn),
        in_specs=[pl.BlockSpec((COND_ROWS, d), lambda l, j: (0, 0)),
                  pl.BlockSpec((1, d, tn), lambda l, j: (l, 0, j)),
                  pl.BlockSpec((1, 1, tn), lambda l, j: (l, 0, j))],
        out_specs=pl.BlockSpec((1, COND_ROWS, tn), lambda l, j: (l, 0, j)),
        out_shape=jax.ShapeDtypeStruct((depth, COND_ROWS, n), jnp.float32),
        compiler_params=pltpu.CompilerParams(dimension_semantics=("arbitrary", "arbitrary"),
                                             vmem_limit_bytes=VMEM_LIMIT_BYTES),
        name="adaln_table")(cond, w_mod, b_mod.reshape(depth, 1, n))


def _in_proj_kernel(x_ref, mod_ref, w_ref, *o_refs):
    m = mod_ref[0]
    h = _ln(x_ref[...]) * (1.0 + m[:, D_MODEL:2 * D_MODEL]) + m[:, 0:D_MODEL]
    p = jnp.dot(h.astype(jnp.bfloat16), w_ref[...], preferred_element_type=jnp.float32)
    off = 0
    for o_ref, w in zip(o_refs, IN_OUT_WIDTHS):
        o_ref[...] = p[:, off:off + w]
        off += w


def _mod_spec(mod, layer, is_ctx, rows_per_step):
    return pl.BlockSpec((None, 1, 1, mod.shape[3]),
                        lambda i, *_: (layer, _cond_row(i, is_ctx, rows_per_step), 0, 0))


def _layer_spec(a, layer):
    return pl.BlockSpec((None,) + a.shape[1:], lambda *_: (layer,) + (0,) * (a.ndim - 1))


def in_proj(x, mod, w_cat, layer, is_ctx):
    T, d = x.shape
    n = w_cat.shape[2]
    return pl.pallas_call(
        _in_proj_kernel, grid=(T // ROW_TILE,),
        in_specs=[pl.BlockSpec((ROW_TILE, d), lambda i: (i, 0)),
                  _mod_spec(mod, layer, is_ctx, ROW_TILE),
                  pl.BlockSpec((None, d, n), lambda i: (layer, 0, 0), pipeline_mode=pl.Buffered(1))],
        out_specs=[pl.BlockSpec((ROW_TILE, w), lambda i: (i, 0)) for w in IN_OUT_WIDTHS],
        out_shape=[jax.ShapeDtypeStruct((T, w), jnp.float32) for w in IN_OUT_WIDTHS],
        compiler_params=pltpu.CompilerParams(dimension_semantics=("arbitrary",), vmem_limit_bytes=VMEM_LIMIT_BYTES),
        name="in_proj")(x, mod, w_cat)


def _dwconv_silu(x, w_ref, b_ref):
    L = x.shape[0]
    row = lax.broadcasted_iota(jnp.int32, x.shape, 0)
    acc = x * w_ref[CONV_W // 2:CONV_W // 2 + 1, :] + b_ref[...]
    for w in range(CONV_W):
        s = w - CONV_W // 2
        if s == 0:
            continue
        shifted = pltpu.roll(x, (-s) % L, 0)
        ok = (row + s >= 0) & (row + s < L)
        acc = acc + jnp.where(ok, shifted, 0.0) * w_ref[w:w + 1, :]
    return _silu(acc)


def _cumsum_rows(a, reverse):
    q = a.shape[0]
    row = lax.broadcasted_iota(jnp.int32, a.shape, 0)
    s = 1
    while s < q:
        if reverse:
            a = a + jnp.where(row < q - s, pltpu.roll(a, q - s, 0), 0.0)
        else:
            a = a + jnp.where(row >= s, pltpu.roll(a, s, 0), 0.0)
        s *= 2
    return a


def _ssd_kernel(z_ref, xs_ref, bc_ref, dt_ref, cwx_ref, cbx_ref, cwbc_ref, cbbc_ref, a_ref, dtb_ref, dsk_ref,
                ng_ref, h0_ref, y_ref, st_ref, xc_ref, bcc_ref, dtp_ref, *, n_chunks):
    Q = SSD_CHUNK
    P = SSD_HEAD_DIM
    N = SSD_STATE
    rep = SSD_HEADS // SSD_GROUPS
    xc_ref[...] = _dwconv_silu(xs_ref[0], cwx_ref, cbx_ref)
    bcc_ref[...] = _dwconv_silu(bc_ref[0], cwbc_ref, cbbc_ref)
    pre = dt_ref[0] + dtb_ref[...]
    dtp_ref[...] = jnp.maximum(pre, 0.0) + jnp.log1p(jnp.exp(-jnp.abs(pre)))
    y_ref[0] = xc_ref[...] * dsk_ref[...]
    st_ref[0] = h0_ref[0]
    ii = lax.broadcasted_iota(jnp.int32, (Q, Q), 0)
    jj = lax.broadcasted_iota(jnp.int32, (Q, Q), 1)
    nt = (((1,), (1,)), ((), ()))

    def chunk_pair(c, carry):
        for d in range(2):
            cc = c if d == 0 else n_chunks - 1 - c
            r0 = pl.multiple_of(cc * Q, Q)
            x = xc_ref[pl.ds(r0, Q), :]
            bcv = bcc_ref[pl.ds(r0, Q), :]
            dt = dtp_ref[pl.ds(r0, Q), :]
            a = dt * a_ref[...]
            cs = _cumsum_rows(a, reverse=(d == 1))
            tot = cs[Q - 1:Q, :] if d == 0 else cs[0:1, :]
            dec = jnp.exp(tot - cs)
            inn = jnp.exp(cs)
            cs_t = cs.T
            w_t = (dt * dec).T
            etot = jnp.exp(tot)
            x_t = x.T
            keep = (jj <= ii) if d == 0 else (jj >= ii)
            ys = []
            for g in range(SSD_GROUPS):
                bg = bcv[:, g * N:(g + 1) * N].astype(jnp.bfloat16)
                cg = bcv[:, SSD_BC + g * N:SSD_BC + (g + 1) * N].astype(jnp.bfloat16)
                cb = lax.dot_general(cg, bg, nt, preferred_element_type=jnp.float32)
                for hg in range(rep):
                    h = g * rep + hg
                    ln = d * SSD_HEADS + h
                    seg = cs[:, ln:ln + 1] - cs_t[ln:ln + 1, :]
                    lmat = jnp.where(keep, jnp.exp(jnp.where(keep, seg, 0.0)), 0.0)
                    xh = x[:, h * P:(h + 1) * P]
                    xd = (xh * dt[:, ln:ln + 1]).astype(jnp.bfloat16)
                    y_diag = jnp.dot((cb * lmat).astype(jnp.bfloat16), xd, preferred_element_type=jnp.float32)
                    hprev = st_ref[0, d, h]
                    y_off = lax.dot_general(cg, hprev.astype(jnp.bfloat16), nt,
                                            preferred_element_type=jnp.float32) * inn[:, ln:ln + 1]
                    ys.append(y_diag + y_off)
                    xw = (x_t[h * P:(h + 1) * P, :] * w_t[ln:ln + 1, :]).astype(jnp.bfloat16)
                    s_new = jnp.dot(xw, bg, preferred_element_type=jnp.float32)
                    st_ref[0, d, h] = hprev * etot[:, ln:ln + 1] + s_new
            y_ref[0, pl.ds(r0, Q), :] += jnp.concatenate(ys, axis=1)
        return carry

    lax.fori_loop(0, n_chunks, chunk_pair, 0)
    y = y_ref[0] * _silu(z_ref[0])
    y = y * lax.rsqrt(jnp.mean(y * y, axis=-1, keepdims=True) + EPS)
    y_ref[0] = y * ng_ref[...]


def ssd_mixer(z, xs, bc, dt_raw, conv_w, conv_b, a_log, dt_bias, d_skip, norm_g, h0, layer=None):
    B, L, _ = z.shape
    n_chunks = L // SSD_CHUNK
    pad = lambda v: jnp.pad(v.reshape(1, -1), ((0, 0), (0, LANES - v.size)))
    a_row = pad(-jnp.exp(a_log))
    dtb_row = pad(dt_bias)
    dsk = jnp.repeat(d_skip, SSD_HEAD_DIM)[None, :]
    seq = lambda w: pl.BlockSpec((1, L, w), lambda b: (b, 0, 0))
    full = lambda a: pl.BlockSpec(a.shape, lambda b: (0,) * a.ndim)
    cwx, cwbc = conv_w[:, :SSD_INNER], conv_w[:, SSD_INNER:]
    cbx, cbbc = conv_b[None, :SSD_INNER], conv_b[None, SSD_INNER:]
    st_shape = (2, SSD_HEADS, SSD_HEAD_DIM, SSD_STATE)
    st_spec = pl.BlockSpec((1,) + st_shape, lambda b: (b, 0, 0, 0, 0))
    h0_spec = st_spec if layer is None else pl.BlockSpec((1, None) + st_shape, lambda b: (b, layer, 0, 0, 0, 0))
    consts = [cwx, cbx, cwbc, cbbc, a_row, dtb_row, dsk, norm_g[None, :]]
    return pl.pallas_call(
        functools.partial(_ssd_kernel, n_chunks=n_chunks),
        grid=(B,),
        in_specs=[seq(SSD_INNER), seq(SSD_INNER), seq(2 * SSD_BC), seq(DT_PAD)] + [full(a) for a in consts]
        + [h0_spec],
        out_specs=[seq(SSD_INNER), st_spec],
        out_shape=[jax.ShapeDtypeStruct((B, L, SSD_INNER), jnp.float32),
                   jax.ShapeDtypeStruct((B,) + st_shape, jnp.float32)],
        scratch_shapes=[pltpu.VMEM((L, SSD_INNER), jnp.float32), pltpu.VMEM((L, 2 * SSD_BC), jnp.float32),
                        pltpu.VMEM((L, DT_PAD), jnp.float32)],
        compiler_params=pltpu.CompilerParams(dimension_semantics=("arbitrary",), vmem_limit_bytes=VMEM_LIMIT_BYTES),
        name="ssd_mixer")(z, xs, bc, dt_raw, *consts, h0)


def _head_rms(x, gain_row, n_heads):
    parts = []
    for h in range(n_heads):
        xh = x[:, h * HEAD_DIM:(h + 1) * HEAD_DIM]
        parts.append(xh * lax.rsqrt(jnp.mean(xh * xh, axis=-1, keepdims=True) + EPS))
    return jnp.concatenate(parts, axis=1) * gain_row


def _rope(x, cos, sin_signed, first16):
    n = x.shape[1]
    partner = jnp.where(first16, pltpu.roll(x, n - 16, 1), pltpu.roll(x, 16, 1))
    return x * cos + partner * sin_signed


def _softmax_rows(s):
    m = jnp.max(s, axis=-1, keepdims=True)
    e = jnp.exp(s - m)
    return e * (1.0 / jnp.sum(e, axis=-1, keepdims=True))


def _dense_attn_kernel(*refs, n_heads, n_kv, has_cache, norm, rope, q_blk):
    it = iter(refs)
    q_ref, k_ref, v_ref = next(it), next(it), next(it)
    ck_ref = cv_ref = qg_ref = kg_ref = cq_ref = sq_ref = ckk_ref = skk_ref = None
    if has_cache:
        ck_ref, cv_ref = next(it), next(it)
    if norm:
        qg_ref, kg_ref = next(it), next(it)
    if rope:
        cq_ref, sq_ref, ckk_ref, skk_ref = next(it), next(it), next(it), next(it)
    o_ref = next(it)
    kn_ref = next(it) if norm else None

    q = q_ref[0]
    k = k_ref[0]
    if norm:
        q = _head_rms(q, qg_ref[...], n_heads)
        k = _head_rms(k, kg_ref[...], n_kv)
        kn_ref[0] = k
    if rope:
        lane_q = lax.broadcasted_iota(jnp.int32, q.shape, 1)
        lane_k = lax.broadcasted_iota(jnp.int32, k.shape, 1)
        q = _rope(q, cq_ref[...], sq_ref[...], lax.bitwise_and(lane_q, 31) < 16)
        k = _rope(k, ckk_ref[...], skk_ref[...], lax.bitwise_and(lane_k, 31) < 16)
    qb = (q * ATTN_SCALE).astype(jnp.bfloat16)
    kb = k.astype(jnp.bfloat16)
    vb = v_ref[0].astype(jnp.bfloat16)
    if has_cache:
        kb = jnp.concatenate([ck_ref[0].astype(jnp.bfloat16), kb], axis=0)
        vb = jnp.concatenate([cv_ref[0].astype(jnp.bfloat16), vb], axis=0)
    lq = q.shape[0]
    grp = n_heads // n_kv
    outs = []
    for h in range(n_heads):
        kv = h // grp
        kh = kb[:, kv * HEAD_DIM:(kv + 1) * HEAD_DIM]
        vh = vb[:, kv * HEAD_DIM:(kv + 1) * HEAD_DIM]
        rows = []
        for r0 in range(0, lq, q_blk):
            qh = qb[r0:r0 + q_blk, h * HEAD_DIM:(h + 1) * HEAD_DIM]
            s = lax.dot_general(qh, kh, (((1,), (1,)), ((), ())), preferred_element_type=jnp.float32)
            p = _softmax_rows(s).astype(jnp.bfloat16)
            rows.append(jnp.dot(p, vh, preferred_element_type=jnp.float32))
        outs.append(rows[0] if len(rows) == 1 else jnp.concatenate(rows, axis=0))
    o_ref[0] = jnp.concatenate(outs, axis=1)


def dense_attention(q, k, v, cache_k=None, cache_v=None, q_gain=None, k_gain=None, rope_tabs=None, layer=0):
    B, L, qw = q.shape
    kw = k.shape[2]
    n_heads, n_kv = qw // HEAD_DIM, kw // HEAD_DIM
    has_cache, norm, rope = cache_k is not None, q_gain is not None, rope_tabs is not None
    args = [q, k, v]
    specs = [pl.BlockSpec((1, L, qw), lambda b: (b, 0, 0)),
             pl.BlockSpec((1, L, kw), lambda b: (b, 0, 0)),
             pl.BlockSpec((1, L, kw), lambda b: (b, 0, 0))]
    if has_cache:
        lc = cache_k.shape[2]
        args += [cache_k, cache_v]
        specs += [pl.BlockSpec((1, None, lc, kw), lambda b: (b, layer, 0, 0))] * 2
    if norm:
        args += [jnp.tile(q_gain, n_heads)[None, :], jnp.tile(k_gain, n_kv)[None, :]]
        specs += [pl.BlockSpec((1, qw), lambda b: (0, 0)), pl.BlockSpec((1, kw), lambda b: (0, 0))]
    if rope:
        args += list(rope_tabs)
        specs += [pl.BlockSpec((L, qw), lambda b: (0, 0))] * 2 + [pl.BlockSpec((L, kw), lambda b: (0, 0))] * 2
    out_shape = [jax.ShapeDtypeStruct((B, L, qw), jnp.float32)]
    out_specs = [pl.BlockSpec((1, L, qw), lambda b: (b, 0, 0))]
    if norm:
        out_shape.append(jax.ShapeDtypeStruct((B, L, kw), jnp.float32))
        out_specs.append(pl.BlockSpec((1, L, kw), lambda b: (b, 0, 0)))
    kern = functools.partial(_dense_attn_kernel, n_heads=n_heads, n_kv=n_kv, has_cache=has_cache, norm=norm,
                             rope=rope, q_blk=min(L, 256))
    res = pl.pallas_call(kern, grid=(B,), in_specs=specs, out_specs=out_specs, out_shape=out_shape,
                         compiler_params=pltpu.CompilerParams(dimension_semantics=("arbitrary",),
                                                              vmem_limit_bytes=VMEM_LIMIT_BYTES),
                         name="dense_attention")(*args)
    return res if norm else res[0]


def rope_tables(L, n_heads):
    t = np.arange(L)
    pos = np.stack([t // GRID_W, t % GRID_W], axis=1).astype(np.float32)
    half = HEAD_DIM // 2
    inv = 1.0 / (ROPE_THETA ** (jnp.arange(0, half, 2, dtype=jnp.float32) / half))
    ang = jnp.asarray(pos)[:, :, None] * inv[None, None, :]
    cos, sin = jnp.cos(ang), jnp.sin(ang)
    cos64 = jnp.concatenate([cos, cos], axis=2).reshape(L, HEAD_DIM)
    sin64 = jnp.concatenate([-sin, sin], axis=2).reshape(L, HEAD_DIM)
    return jnp.tile(cos64, (1, n_heads)), jnp.tile(sin64, (1, n_heads))


def na_bias_table(rpb):
    qcol = np.arange(GRID_W)[:, None]
    kc = np.arange(GRID_W)[None, :]
    qstart = np.clip(qcol - NA_COLS // 2, 0, GRID_W - NA_COLS)
    valid = (kc >= qstart) & (kc < qstart + NA_COLS)
    dc = np.clip(kc - qcol + NA_COLS - 1, 0, 2 * NA_COLS - 2)
    t = jnp.where(jnp.asarray(valid), rpb[..., dc], NEG_BIG)
    return jnp.concatenate([t[..., :-1, :, :], t[..., 1:, :, :]], axis=-1)


def _na_kernel(q_ref, k_ref, v_ref, ck_ref, cv_ref, bias_ref, o_ref, s_ref, p_ref, *, rows):
    qb = (q_ref[0] * ATTN_SCALE).astype(jnp.bfloat16)
    kb = k_ref[0].astype(jnp.bfloat16)
    vb = v_ref[0].astype(jnp.bfloat16)
    ckb = ck_ref[0].astype(jnp.bfloat16)
    cvb = cv_ref[0].astype(jnp.bfloat16)
    n_loc = NA_ROWS * GRID_W
    L = qb.shape[0]
    nt = (((1,), (1,)), ((), ()))
    key_row0 = [min(max(r - NA_ROWS // 2, 0), rows - NA_ROWS) for r in range(rows)]
    outs = []
    for hh in range(2):
        sl = slice(hh * HEAD_DIM, (hh + 1) * HEAD_DIM)
        qh = qb[:, sl]
        s_ref[:, n_loc:] = lax.dot_general(qh, ckb[:, sl], nt, preferred_element_type=jnp.float32)
        for r, kr0 in enumerate(key_row0):
            kl = kb[kr0 * GRID_W:kr0 * GRID_W + n_loc, sl]
            s_loc = lax.dot_general(qh[r * GRID_W:(r + 1) * GRID_W], kl, nt, preferred_element_type=jnp.float32)
            dr0 = kr0 - r + NA_ROWS - 1
            for c in range(NA_ROWS // 2):
                s_ref[r * GRID_W:(r + 1) * GRID_W, c * LANES:(c + 1) * LANES] = (
                    s_loc[:, c * LANES:(c + 1) * LANES] + bias_ref[hh, dr0 + 2 * c])
        for r0 in range(0, L, NA_SOFTMAX_ROWS):
            p_ref[r0:r0 + NA_SOFTMAX_ROWS, :] = _softmax_rows(s_ref[r0:r0 + NA_SOFTMAX_ROWS, :]).astype(jnp.bfloat16)
        o_ctx = jnp.dot(p_ref[:, n_loc:], cvb[:, sl], preferred_element_type=jnp.float32)
        o_loc = []
        for r, kr0 in enumerate(key_row0):
            vl = vb[kr0 * GRID_W:kr0 * GRID_W + n_loc, sl]
            o_loc.append(jnp.dot(p_ref[r * GRID_W:(r + 1) * GRID_W, :n_loc], vl, preferred_element_type=jnp.float32))
        outs.append(o_ctx + jnp.concatenate(o_loc, axis=0))
    o_ref[0] = jnp.concatenate(outs, axis=1)


def neighbourhood_attention(q, k, v, ctx_k, ctx_v, bias_tbl, layer):
    B, L, w = q.shape
    lc = ctx_k.shape[2]
    n_pairs = w // LANES
    rows = L // GRID_W
    assert rows >= NA_ROWS
    kern = functools.partial(_na_kernel, rows=rows)
    blk = lambda n: pl.BlockSpec((1, n, LANES), lambda hp, b: (b, 0, hp))
    cache_blk = pl.BlockSpec((1, None, lc, LANES), lambda hp, b: (b, layer, 0, hp))
    return pl.pallas_call(
        kern, grid=(n_pairs, B),
        in_specs=[blk(L), blk(L), blk(L), cache_blk, cache_blk,
                  pl.BlockSpec((None, 2) + bias_tbl.shape[2:], lambda hp, b: (layer, hp, 0, 0, 0))],
        out_specs=blk(L),
        out_shape=jax.ShapeDtypeStruct((B, L, w), jnp.float32),
        scratch_shapes=[pltpu.VMEM((L, NA_ROWS * GRID_W + lc), jnp.float32),
                        pltpu.VMEM((L, NA_ROWS * GRID_W + lc), jnp.bfloat16)],
        compiler_params=pltpu.CompilerParams(dimension_semantics=("arbitrary", "arbitrary"),
                                             vmem_limit_bytes=VMEM_LIMIT_BYTES),
        name="neighbourhood_attention")(q, k, v, ctx_k, ctx_v, bias_tbl)


def _merge_kernel(ya_ref, yb_ref, yc_ref, gate_ref, x_ref, mod_ref, wb_ref, wo_ref, g_ref, b_ref, wqt_ref,
                  x1_ref, h2_ref, qt_ref):
    m = mod_ref[0]
    merged = None
    for i, y_ref in enumerate((ya_ref, yb_ref, yc_ref)):
        br = jnp.dot(y_ref[...].astype(jnp.bfloat16), wb_ref[i], preferred_element_type=jnp.float32)
        t = _sigmoid(gate_ref[:, i * D_MODEL:(i + 1) * D_MODEL]) * br
        merged = t if merged is None else merged + t
    y = jnp.dot(merged.astype(jnp.bfloat16), wo_ref[...], preferred_element_type=jnp.float32)
    x1 = _ln(DN_ALPHA * x_ref[...] + m[:, 2 * D_MODEL:3 * D_MODEL] * y) * g_ref[...] + b_ref[...]
    x1_ref[...] = x1
    h2 = (_ln(x1) * (1.0 + m[:, 4 * D_MODEL:5 * D_MODEL]) + m[:, 3 * D_MODEL:4 * D_MODEL]).astype(jnp.bfloat16)
    h2_ref[...] = h2
    qt = lax.dot_general(wqt_ref[...], h2, (((1,), (1,)), ((), ())),
                         preferred_element_type=jnp.float32).astype(jnp.bfloat16)
    for u in range(ROW_TILE // TOPK_TOK):
        qt_ref[u] = qt[:, u * TOPK_TOK:(u + 1) * TOPK_TOK]


def merge_branches(ya, yb, yc, gate, x, mod, w_branch, w_out, ln_g, ln_b, wq_t, layer, is_ctx):
    T, d = x.shape
    nq = wq_t.shape[1]
    row = lambda w: pl.BlockSpec((ROW_TILE, w), lambda i: (i, 0))
    full = lambda a: _layer_spec(a, layer)
    return pl.pallas_call(
        _merge_kernel, grid=(T // ROW_TILE,),
        in_specs=[row(512), row(512), row(512), row(N_BRANCH * d), row(d),
                  _mod_spec(mod, layer, is_ctx, ROW_TILE),
                  full(w_branch), full(w_out), full(ln_g), full(ln_b), full(wq_t)],
        out_specs=[row(d), row(d), pl.BlockSpec((ROW_TILE // TOPK_TOK, nq, TOPK_TOK), lambda i: (i, 0, 0))],
        out_shape=[jax.ShapeDtypeStruct((T, d), jnp.float32), jax.ShapeDtypeStruct((T, d), jnp.bfloat16),
                   jax.ShapeDtypeStruct((T // TOPK_TOK, nq, TOPK_TOK), jnp.bfloat16)],
        compiler_params=pltpu.CompilerParams(dimension_semantics=("arbitrary",), vmem_limit_bytes=VMEM_LIMIT_BYTES),
        name="merge_branches")(ya, yb, yc, gate, x, mod, w_branch, w_out, ln_g, ln_b, wq_t)


CAND_GROUPS = ((0, 1, 16), (1, 1, 8), (2, 1, 8), (3, 1, 8), (4, 1, 8), (5, 1, 8), (6, 1, 8), (7, 1, 8), (8, 8, 1))


def _cand_flat_index():
    rows = []
    for a0, na, nb in CAND_GROUPS:
        rows += [a * PEER_TOPK + b for a in range(a0, a0 + na) for b in range(nb)]
    needed = {(a, b) for a in range(PEER_TOPK) for b in range(PEER_TOPK) if (a + 1) * (b + 1) <= PEER_TOPK}
    assert needed <= {(r // PEER_TOPK, r % PEER_TOPK) for r in rows}
    return np.asarray(rows, np.float32)


def _topk_rows(s, ids, k):
    vals, idxs = [], []
    big = jnp.float32(1e9)
    for _ in range(k):
        m = jnp.max(s, axis=0, keepdims=True)
        idx = jnp.min(jnp.where(s == m, ids, big), axis=0, keepdims=True)
        vals.append(m)
        idxs.append(idx)
        s = jnp.where(ids == idx, -jnp.inf, s)
    return jnp.concatenate(vals, axis=0), jnp.concatenate(idxs, axis=0)


def _select_rows(table, sel):
    out = jnp.zeros(sel.shape, table.dtype)
    for a in range(table.shape[0]):
        out = jnp.where(sel == float(a), table[a:a + 1, :], out)
    return out


def _topk_head(qt_ref, sb, h, keys_ref, key_ids, cid):
    sv, si = [], []
    for p in range(2):
        q = qt_ref[sb, pl.ds(pl.multiple_of(h * 2 * PEER_KEYS + p * PEER_KEYS, PEER_KEYS), PEER_KEYS), :]
        s = jnp.dot(keys_ref[h, p], q, preferred_element_type=jnp.float32)
        v, i = _topk_rows(s, key_ids, PEER_TOPK)
        sv.append(v)
        si.append(i)
    groups = []
    for a0, na, nb in CAND_GROUPS:
        groups.append(sv[0][a0:a0 + na, :] + sv[1][0:nb, :])
    cand = jnp.concatenate(groups, axis=0)
    cv, ci = _topk_rows(cand, cid, PEER_TOPK)
    a_sel = jnp.floor(ci * (1.0 / PEER_TOPK))
    b_sel = ci - a_sel * PEER_TOPK
    i_sel = _select_rows(si[0], a_sel)
    j_sel = _select_rows(si[1], b_sel)
    ex = jnp.exp(cv - cv[0:1, :])
    g = ex / jnp.sum(ex, axis=0, keepdims=True)
    return i_sel * PEER_KEYS + j_sel, g


def _key_ids():
    return lax.broadcasted_iota(jnp.int32, (PEER_KEYS, TOPK_TOK), 0).astype(jnp.float32)


def _peer_topk_kernel(qt_ref, keys_ref, cid_ref, et_ref, gt_ref):
    key_ids = _key_ids()

    def head_group(hg, carry):
        for u in range(TOPK_HEADS_PER_BLOCK):
            h = TOPK_HEADS_PER_BLOCK * hg + u
            e, g = _topk_head(qt_ref, 0, h, keys_ref, key_ids, cid_ref[...])
            row = pl.multiple_of(h * PEER_TOPK, PEER_TOPK)
            et_ref[0, pl.ds(row, PEER_TOPK), :] = e
            gt_ref[0, pl.ds(row, PEER_TOPK), :] = g
        return carry

    lax.fori_loop(0, PEER_HEADS // TOPK_HEADS_PER_BLOCK, head_group, 0)


def _cand_ids():
    return jnp.asarray(np.tile(_cand_flat_index()[:, None], (1, TOPK_TOK)))


def peer_topk_first_block(qt3, keys_bf16, layer):
    cid = _cand_ids()
    shape = (PEER_SUB_BLOCKS, PEER_HEADS * PEER_TOPK, TOPK_TOK)
    blk = pl.BlockSpec((1,) + shape[1:], lambda s: (s, 0, 0))
    return pl.pallas_call(
        _peer_topk_kernel,
        grid=(PEER_SUB_BLOCKS,),
        in_specs=[pl.BlockSpec((1,) + qt3.shape[1:], lambda s: (s, 0, 0)),
                  _layer_spec(keys_bf16, layer),
                  pl.BlockSpec(cid.shape, lambda s: (0, 0))],
        out_specs=[blk, blk],
        out_shape=[jax.ShapeDtypeStruct(shape, jnp.float32)] * 2,
        compiler_params=pltpu.CompilerParams(dimension_semantics=("arbitrary",)),
        name="peer_topk",
    )(qt3, keys_bf16, cid)


def _peer_dense_kernel(x_ref, qt_ref, keys_ref, cid_ref, et0_ref, gt0_ref, ut_ref, v_ref, x1_ref, mod_ref, lg_ref,
                       lb_ref, o_ref, g3_ref, acc_ref, ets_ref, gts_ref, erow_ref, grow_ref, *, t_blk, te):
    i = pl.program_id(0)
    j = pl.program_id(1)
    nk = PEER_KEYS
    n_i = te // nk
    slot = lax.rem(i, 2)

    @pl.when(j == 0)
    def _build():
        @pl.when(i == 0)
        def _first():
            ets_ref[0] = et0_ref[...]
            gts_ref[0] = gt0_ref[...]

        for sb in range(PEER_SUB_BLOCKS):
            erow_ref[sb * TOPK_TOK:(sb + 1) * TOPK_TOK, :] = ets_ref[slot, sb].T
            grow_ref[sb * TOPK_TOK:(sb + 1) * TOPK_TOK, :] = gts_ref[slot, sb].T
        acc_ref[...] = jnp.zeros_like(acc_ref)
        sub_iota = lax.broadcasted_iota(jnp.int32, (nk, nk), 0)
        zeros = jnp.zeros((nk, nk), jnp.bfloat16)

        def body(tp, carry):
            onehot_i, gated_j = [], []
            for u in range(2):
                e_row = erow_ref[pl.ds(2 * tp + u, 1), :].astype(jnp.int32)
                g_row = grow_ref[pl.ds(2 * tp + u, 1), :]
                i_row = lax.shift_right_logical(e_row, 7)
                j_row = lax.bitwise_and(e_row, nk - 1)
                onehot_i.append(jnp.where(sub_iota == i_row, 1.0, 0.0).astype(jnp.bfloat16))
                gated_j.append(jnp.where(sub_iota == j_row, g_row, 0.0).astype(jnp.bfloat16))
            lhs = jnp.concatenate(onehot_i, axis=1)
            rhs = jnp.concatenate([jnp.concatenate([gated_j[0], zeros], axis=1),
                                   jnp.concatenate([zeros, gated_j[1]], axis=1)], axis=0)
            gt = lax.dot_general(lhs, rhs, (((1,), (1,)), ((), ())), preferred_element_type=jnp.float32)
            for u in range(2):
                row0 = pl.multiple_of((2 * tp + u) * G3_PITCH, 8)
                g3_ref[pl.ds(row0, nk), :] = gt[:, u * nk:(u + 1) * nk]
            return carry

        lax.fori_loop(0, t_blk // 2, body, 0, unroll=16)

    sb = j // PEER_HEADS
    h = lax.rem(j, PEER_HEADS)
    e_next, g_next = _topk_head(qt_ref, sb, h, keys_ref, _key_ids(), cid_ref[...])
    row = pl.multiple_of(h * PEER_TOPK, PEER_TOPK)
    ets_ref[1 - slot, sb, pl.ds(row, PEER_TOPK), :] = e_next
    gts_ref[1 - slot, sb, pl.ds(row, PEER_TOPK), :] = g_next

    s = jnp.dot(x_ref[...], ut_ref[...], preferred_element_type=jnp.float32)
    parts = []
    for k in range(n_i):
        gi = g3_ref[pl.ds(j * n_i + k, t_blk, stride=G3_PITCH), :]
        parts.append((_gelu_tanh(s[:, k * nk:(k + 1) * nk]) * gi).astype(jnp.bfloat16))
    a = jnp.concatenate(parts, axis=1)
    acc_ref[...] += jnp.dot(a, v_ref[...], preferred_element_type=jnp.float32)

    @pl.when(j == pl.num_programs(1) - 1)
    def _fin():
        gate2 = mod_ref[0][:, 5 * D_MODEL:6 * D_MODEL]
        o_ref[...] = _ln(DN_ALPHA * x1_ref[...] + gate2 * acc_ref[...]) * lg_ref[...] + lb_ref[...]


def peer_dense(x_bf16, qt3, keys_bf16, et0, gt0, ut_bf16, v_bf16, x1, mod, ln_g, ln_b, layer, is_ctx):
    T, D = x_bf16.shape
    NE = v_bf16.shape[1]
    t_blk, te = PEER_TOK_BLOCK, PEER_EXPERT_TILE
    n_blk = T // t_blk
    assert NE // te == PEER_SUB_BLOCKS * PEER_HEADS
    cid = _cand_ids()
    kern = functools.partial(_peer_dense_kernel, t_blk=t_blk, te=te)
    single = dict(pipeline_mode=pl.Buffered(1))
    tok = lambda w, **kw: pl.BlockSpec((t_blk, w), lambda i, j: (i, 0), **kw)
    sel_shape = (PEER_SUB_BLOCKS, PEER_HEADS * PEER_TOPK, TOPK_TOK)
    whole = lambda a: pl.BlockSpec(a.shape, lambda i, j: (0,) * a.ndim)
    return pl.pallas_call(
        kern,
        grid=(n_blk, NE // te),
        in_specs=[tok(D, **single),
                  pl.BlockSpec((PEER_SUB_BLOCKS,) + qt3.shape[1:],
                               lambda i, j: (jnp.minimum(i + 1, n_blk - 1), 0, 0), **single),
                  pl.BlockSpec((None,) + keys_bf16.shape[1:], lambda i, j: (layer, 0, 0, 0, 0), **single),
                  whole(cid), whole(et0), whole(gt0),
                  pl.BlockSpec((None, D, te), lambda i, j: (layer, 0, j)),
                  pl.BlockSpec((None, te, D), lambda i, j: (layer, j, 0)),
                  tok(D, **single),
                  _mod_spec(mod, layer, is_ctx, t_blk),
                  _layer_spec(ln_g, layer),
                  _layer_spec(ln_b, layer)],
        out_specs=tok(D),
        out_shape=jax.ShapeDtypeStruct((T, D), jnp.float32),
        scratch_shapes=[pltpu.VMEM((t_blk * G3_PITCH, PEER_KEYS), jnp.float32),
                        pltpu.VMEM((t_blk, D), jnp.float32),
                        pltpu.VMEM((2,) + sel_shape, jnp.float32),
                        pltpu.VMEM((2,) + sel_shape, jnp.float32),
                        pltpu.VMEM((t_blk, LANES), jnp.float32),
                        pltpu.VMEM((t_blk, LANES), jnp.float32)],
        compiler_params=pltpu.CompilerParams(dimension_semantics=("arbitrary", "arbitrary"),
                                             vmem_limit_bytes=VMEM_LIMIT_BYTES),
        name="peer_dense",
    )(x_bf16, qt3, keys_bf16, cid, et0, gt0, ut_bf16, v_bf16, x1, mod, ln_g, ln_b)


def _in_proj_weight(w_in):
    dt0 = sum(IN_SIZES[:4])
    dt1 = dt0 + IN_SIZES[4]
    pad = jnp.zeros(w_in.shape[:2] + (DT_PAD - IN_SIZES[4],), w_in.dtype)
    return jnp.concatenate([w_in[..., :dt1], pad, w_in[..., dt1:]], axis=-1).astype(jnp.bfloat16)


def kernel(x_prompt, x_sample, cache_na_k, cache_na_v, cache_gqa_k, cache_gqa_v, state_ssd, c, c_ctx,
           w_mod, b_mod, w_in, conv_w, conv_b, ssd_a_log, ssd_dt_bias, ssd_d, ssd_norm_g, na_rpb,
           gqa_q_norm, gqa_k_norm, w_branch, w_out, ln1_g, ln1_b, ln2_g, ln2_b,
           peer_wq, peer_keys, peer_u, peer_v):
    bf = jnp.bfloat16
    cond = jnp.zeros((COND_ROWS, D_MODEL), jnp.float32).at[0].set(c_ctx).at[1:1 + DEC_BATCH].set(c)
    mod = adaln_table(cond, w_mod, b_mod).reshape(DEPTH, COND_ROWS, 1, 6 * D_MODEL)
    bias_tbl = na_bias_table(na_rpb)
    w_cat = _in_proj_weight(w_in)
    wb, wo, wq_t = w_branch.astype(bf), w_out.astype(bf), jnp.swapaxes(peer_wq, 1, 2).astype(bf)
    keys_b = peer_keys.astype(bf)
    ut, vb = jnp.swapaxes(peer_u, 1, 2).astype(bf), peer_v.astype(bf)
    ln1 = (ln1_g[:, None, :], ln1_b[:, None, :])
    ln2 = (ln2_g[:, None, :], ln2_b[:, None, :])
    rope_q = rope_tables(DEC_SEQ, GQA_HEADS)
    rope_k = rope_tables(DEC_SEQ, GQA_KV_HEADS)
    zero_state = jnp.zeros((BATCH, 2, SSD_HEADS, SSD_HEAD_DIM, SSD_STATE), jnp.float32)
    na_ck = cache_na_k.reshape(DEC_BATCH, DEPTH, PAST_LEN, NA_HEADS * HEAD_DIM)
    na_cv = cache_na_v.reshape(DEC_BATCH, DEPTH, PAST_LEN, NA_HEADS * HEAD_DIM)
    gqa_ck = cache_gqa_k.reshape(DEC_BATCH, DEPTH, PAST_LEN, GQA_KV_HEADS * HEAD_DIM)
    gqa_cv = cache_gqa_v.reshape(DEC_BATCH, DEPTH, PAST_LEN, GQA_KV_HEADS * HEAD_DIM)

    streams = {True: x_prompt.reshape(BATCH * SEQ, D_MODEL), False: x_sample.reshape(DEC_BATCH * DEC_SEQ, D_MODEL)}
    new_na_k, new_na_v, new_g_k, new_g_v, new_ssd = [], [], [], [], []
    for l in range(DEPTH):
        ssd_args = (conv_w[l], conv_b[l], ssd_a_log[l], ssd_dt_bias[l], ssd_d[l], ssd_norm_g[l])
        for is_ctx in (True, False):
            x = streams[is_ctx]
            nb, sl = (BATCH, SEQ) if is_ctx else (DEC_BATCH, DEC_SEQ)
            z, xs, bc, dt, naq, nak, nav, gq, gk, gv, gate = in_proj(x, mod, w_cat, l, is_ctx)
            r3 = lambda a: a.reshape(nb, sl, a.shape[-1])
            if is_ctx:
                ya, st = ssd_mixer(r3(z), r3(xs), r3(bc), r3(dt), *ssd_args, zero_state)
                yb = dense_attention(r3(naq), r3(nak), r3(nav))
                yc, gk_n = dense_attention(r3(gq), r3(gk), r3(gv), q_gain=gqa_q_norm[l], k_gain=gqa_k_norm[l])
                new_na_k.append(nak.reshape(BATCH, SEQ, NA_HEADS, HEAD_DIM))
                new_na_v.append(nav.reshape(BATCH, SEQ, NA_HEADS, HEAD_DIM))
                new_g_k.append(gk_n.reshape(BATCH, SEQ, GQA_KV_HEADS, HEAD_DIM))
                new_g_v.append(gv.reshape(BATCH, SEQ, GQA_KV_HEADS, HEAD_DIM))
                new_ssd.append(st)
            else:
                ya, _ = ssd_mixer(r3(z), r3(xs), r3(bc), r3(dt), *ssd_args, state_ssd, layer=l)
                yb = neighbourhood_attention(r3(naq), r3(nak), r3(nav), na_ck, na_cv, bias_tbl, l)
                yc, _ = dense_attention(r3(gq), r3(gk), r3(gv), gqa_ck, gqa_cv,
                                        gqa_q_norm[l], gqa_k_norm[l], rope_q + rope_k, layer=l)
            r2 = lambda a: a.reshape(nb * sl, a.shape[-1])
            x1, h2, qt = merge_branches(r2(ya), r2(yb), r2(yc), gate, x, mod, wb, wo, *ln1, wq_t, l, is_ctx)
            et0, gt0 = peer_topk_first_block(qt, keys_b, l)
            streams[is_ctx] = peer_dense(h2, qt, keys_b, et0, gt0, ut, vb, x1, mod, *ln2, l, is_ctx)

    return (streams[True].reshape(BATCH, SEQ, D_MODEL), streams[False].reshape(DEC_BATCH, DEC_SEQ, D_MODEL),
            jnp.stack(new_na_k, axis=1), jnp.stack(new_na_v, axis=1),
            jnp.stack(new_g_k, axis=1), jnp.stack(new_g_v, axis=1), jnp.stack(new_ssd, axis=1))
```
